```python
import math
import jax, jax.numpy as jnp
from jax import lax
import numpy as np

D_MODEL = 1024
BATCH = 8
SEQ = 4096
DEPTH = 2

RG_WIDTH = D_MODEL // 2
RG_BLOCKS = 8
RG_BLOCK = RG_WIDTH // RG_BLOCKS
CONV_WIDTH = 4
RG_C = 8.0
GLA_HEADS = 4
GLA_VDIM = D_MODEL // 2
GLA_KDIM = GLA_VDIM // 2
GLA_HK = GLA_KDIM // GLA_HEADS
GLA_HV = GLA_VDIM // GLA_HEADS
GLA_LOWRANK = 16
GLA_GATE_TAU = 16.0
GLA_CHUNK = 64
EVEN_IN = 2 * RG_WIDTH + 2 * GLA_KDIM + 2 * GLA_VDIM + GLA_LOWRANK
MIX_WIDTH = RG_WIDTH + GLA_VDIM

FOX_HEADS = 16
FOX_HD = D_MODEL // FOX_HEADS
FOX_BLOCK = 128
ODD_IN = 4 * D_MODEL + FOX_HEADS

D_FF = 4 * D_MODEL
N_EVEN = (DEPTH + 1) // 2
N_ODD = DEPTH // 2
ALPHA = (2 * DEPTH) ** 0.25
BETA = (8 * DEPTH) ** -0.25
LN_EPS = 1e-5
RMS_EPS = 1e-6

kernel_name = "hybrid_rglru_gla_fox_deepnorm_adaln"


def layer_norm(x, g, b):
    xf = x.astype(jnp.float32)
    mu = jnp.mean(xf, axis=-1, keepdims=True)
    var = jnp.mean(jnp.square(xf - mu), axis=-1, keepdims=True)
    return ((xf - mu) * lax.rsqrt(var + LN_EPS) * g.astype(jnp.float32)
            + b.astype(jnp.float32)).astype(x.dtype)


def rms_norm(x, g):
    xf = x.astype(jnp.float32)
    y = xf * lax.rsqrt(jnp.mean(xf * xf, axis=-1, keepdims=True) + RMS_EPS)
    return (y * g.astype(jnp.float32)).astype(x.dtype)


def ada_mod(c, w, b):
    m = (jax.nn.silu(c) @ w + b)[:, None, :]
    shift, scale, gate = jnp.split(m, 3, axis=-1)
    return shift, scale, 1.0 + gate


def causal_dwconv(x, w, b):
    k_len, ch = w.shape
    y = lax.conv_general_dilated(
        x, w[:, None, :].astype(x.dtype), window_strides=(1,),
        padding=[(k_len - 1, 0)], dimension_numbers=('NWC', 'WIO', 'NWC'),
        feature_group_count=ch)
    return y + b


def rg_lru(x, w_a, b_a, w_x, b_x, lam):
    bn, sn, _ = x.shape
    xf = x.astype(jnp.float32)
    xb = xf.reshape(bn, sn, RG_BLOCKS, RG_BLOCK)
    r = jax.nn.sigmoid(jnp.einsum('bsni,nij->bsnj', xb, w_a.astype(jnp.float32)).reshape(bn, sn, RG_WIDTH) + b_a)
    i = jax.nn.sigmoid(jnp.einsum('bsni,nij->bsnj', xb, w_x.astype(jnp.float32)).reshape(bn, sn, RG_WIDTH) + b_x)
    log_a = RG_C * r * jax.nn.log_sigmoid(lam.astype(jnp.float32))
    a = jnp.exp(log_a)
    u = jnp.sqrt(-jnp.expm1(2.0 * log_a)) * (i * xf)

    def combine(left, right):
        a1, h1 = left
        a2, h2 = right
        return a1 * a2, a2 * h1 + h2

    _, h = lax.associative_scan(combine, (a, u), axis=1)
    return h.astype(x.dtype)


def gla_chunked(q, k, v, log_alpha):
    bn, sn, nh, dk = q.shape
    dv = v.shape[-1]
    n_chunks = sn // GLA_CHUNK

    def blocks(t):
        return t.astype(jnp.float32).reshape(bn, n_chunks, GLA_CHUNK, nh, t.shape[-1]).transpose(1, 0, 3, 2, 4)

    qc = blocks(q) * (dk ** -0.5)
    kc = blocks(k)
    vc = blocks(v)
    bcum = jnp.cumsum(blocks(log_alpha), axis=3)
    b_last = bcum[:, :, :, -1:, :]
    q_dec = qc * jnp.exp(bcum)
    k_dec = kc * jnp.exp(-bcum)
    k_end = kc * jnp.exp(b_last - bcum)
    causal = jnp.tril(jnp.ones((GLA_CHUNK, GLA_CHUNK), dtype=bool))
    att = jnp.where(causal, jnp.einsum('nbhtd,nbhsd->nbhts', q_dec, k_dec), 0.0)
    o_intra = jnp.einsum('nbhts,nbhsv->nbhtv', att, vc)
    kv = jnp.einsum('nbhsd,nbhsv->nbhdv', k_end, vc)
    decay = jnp.exp(b_last[:, :, :, 0, :])

    def step(state, inp):
        dcy, kv_n = inp
        return dcy[..., None] * state + kv_n, state

    _, s_prev = lax.scan(step, jnp.zeros((bn, nh, dk, dv), jnp.float32), (decay, kv))
    o = o_intra + jnp.einsum('nbhtd,nbhdv->nbhtv', q_dec, s_prev)
    return o.transpose(1, 0, 3, 2, 4).reshape(bn, sn, nh, dv).astype(q.dtype)


def even_mixer(u, w_in, conv_w, conv_b, rg_wa, rg_ba, rg_wx, rg_bx, rg_lam,
               gla_w_up, gla_b_up, gla_norm_g, w_out):
    bn, sn, _ = u.shape
    proj = u @ w_in
    s1 = RG_WIDTH
    s2 = s1 + RG_WIDTH
    s3 = s2 + GLA_KDIM
    s4 = s3 + GLA_KDIM
    s5 = s4 + GLA_VDIM
    s6 = s5 + GLA_VDIM
    xr, yr, q, k, v, g, zl = jnp.split(proj, [s1, s2, s3, s4, s5, s6], axis=-1)
    h = rg_lru(causal_dwconv(xr, conv_w, conv_b), rg_wa, rg_ba, rg_wx, rg_bx, rg_lam)
    rg_out = h * jax.nn.gelu(yr)
    log_alpha = jax.nn.log_sigmoid((zl @ gla_w_up + gla_b_up).astype(jnp.float32)) / GLA_GATE_TAU
    o = gla_chunked(q.reshape(bn, sn, GLA_HEADS, GLA_HK),
                    k.reshape(bn, sn, GLA_HEADS, GLA_HK),
                    v.reshape(bn, sn, GLA_HEADS, GLA_HV),
                    log_alpha.reshape(bn, sn, GLA_HEADS, GLA_HK))
    o = rms_norm(o, gla_norm_g.reshape(GLA_HEADS, GLA_HV)) * jax.nn.silu(g).reshape(bn, sn, GLA_HEADS, GLA_HV)
    mix = jnp.concatenate([rg_out, o.reshape(bn, sn, GLA_VDIM)], axis=-1)
    return mix @ w_out


def fox_attention(u, w_in, b_f, q_norm_g, k_norm_g, w_out):
    bn, sn, _ = u.shape
    proj = u @ w_in
    q, k, v, g, fl = jnp.split(proj, [D_MODEL, 2 * D_MODEL, 3 * D_MODEL, 4 * D_MODEL], axis=-1)
    q = rms_norm(q.reshape(bn, sn, FOX_HEADS, FOX_HD), q_norm_g).transpose(0, 2, 1, 3) * (FOX_HD ** -0.5)
    k = rms_norm(k.reshape(bn, sn, FOX_HEADS, FOX_HD), k_norm_g).transpose(0, 2, 1, 3)
    v = v.reshape(bn, sn, FOX_HEADS, FOX_HD).transpose(0, 2, 1, 3)
    log_f = jax.nn.log_sigmoid((fl + b_f).astype(jnp.float32))
    f_cum = jnp.cumsum(log_f, axis=1).transpose(0, 2, 1)
    outs = []
    for blk in range(sn // FOX_BLOCK):
        t0 = blk * FOX_BLOCK
        t1 = t0 + FOX_BLOCK
        s = jnp.einsum('bhtd,bhsd->bhts', q[:, :, t0:t1], k[:, :, :t1]).astype(jnp.float32)
        s = s + f_cum[:, :, t0:t1, None] - f_cum[:, :, None, :t1]
        mask = (t0 + jnp.arange(FOX_BLOCK))[:, None] >= jnp.arange(t1)[None, :]
        p = jax.nn.softmax(jnp.where(mask, s, -jnp.inf), axis=-1)
        outs.append(jnp.einsum('bhts,bhsd->bhtd', p.astype(v.dtype), v[:, :, :t1]))
    o = jnp.concatenate(outs, axis=2).transpose(0, 2, 1, 3).reshape(bn, sn, D_MODEL)
    return (o * jax.nn.sigmoid(g)) @ w_out


def setup_inputs(seed: int = 0) -> dict:
    key = jax.random.key(seed)
    ks = jax.random.split(key, 32)
    D = D_MODEL

    def nrm(i, shape, scale):
        return scale * jax.random.normal(ks[i], shape, jnp.float32)

    lam_a = jax.random.uniform(ks[13], (N_EVEN, RG_WIDTH), jnp.float32, 0.9, 0.999)
    lam_root = lam_a ** (1.0 / RG_C)
    ev_lam = jnp.log(lam_root) - jnp.log1p(-lam_root)
    return {
        "x": nrm(0, (BATCH, SEQ, D), 1.0),
        "c": nrm(1, (BATCH, D), 1.0),
        "ada_w": nrm(2, (DEPTH, 2, D, 3 * D), 0.1 * D ** -0.5),
        "ada_b": nrm(3, (DEPTH, 2, 3 * D), 0.02),
        "ln_g": 1.0 + nrm(4, (DEPTH, 2, D), 0.05),
        "ln_b": nrm(5, (DEPTH, 2, D), 0.02),
        "ev_w_in": nrm(6, (N_EVEN, D, EVEN_IN), D ** -0.5),
        "ev_conv_w": nrm(7, (N_EVEN, CONV_WIDTH, RG_WIDTH), CONV_WIDTH ** -0.5),
        "ev_conv_b": nrm(8, (N_EVEN, RG_WIDTH), 0.02),
        "ev_rg_wa": nrm(9, (N_EVEN, RG_BLOCKS, RG_BLOCK, RG_BLOCK), RG_BLOCK ** -0.5),
        "ev_rg_ba": nrm(10, (N_EVEN, RG_WIDTH), 0.02),
        "ev_rg_wx": nrm(11, (N_EVEN, RG_BLOCKS, RG_BLOCK, RG_BLOCK), RG_BLOCK ** -0.5),
        "ev_rg_bx": nrm(12, (N_EVEN, RG_WIDTH), 0.02),
        "ev_rg_lam": ev_lam,
        "ev_gla_w_up": nrm(14, (N_EVEN, GLA_LOWRANK, GLA_KDIM), GLA_LOWRANK ** -0.5),
        "ev_gla_b_up": nrm(15, (N_EVEN, GLA_KDIM), 0.02),
        "ev_gla_norm_g": 1.0 + nrm(16, (N_EVEN, GLA_VDIM), 0.05),
        "ev_w_out": nrm(17, (N_EVEN, MIX_WIDTH, D), BETA * MIX_WIDTH ** -0.5),
        "od_w_in": nrm(18, (N_ODD, D, ODD_IN), D ** -0.5),
        "od_b_f": 3.0 + 3.0 * jax.random.uniform(ks[19], (N_ODD, FOX_HEADS), jnp.float32),
        "od_q_norm_g": 1.0 + nrm(20, (N_ODD, FOX_HD), 0.05),
        "od_k_norm_g": 1.0 + nrm(21, (N_ODD, FOX_HD), 0.05),
        "od_w_out": nrm(22, (N_ODD, D, D), BETA * D ** -0.5),
        "mlp_w1": nrm(23, (DEPTH, D, D_FF), D ** -0.5),
        "mlp_b1": nrm(24, (DEPTH, D_FF), 0.02),
        "mlp_w2": nrm(25, (DEPTH, D_FF, D), BETA * D_FF ** -0.5),
        "mlp_b2": nrm(26, (DEPTH, D), 0.02),
    }


def reference(x, c, ada_w, ada_b, ln_g, ln_b,
              ev_w_in, ev_conv_w, ev_conv_b, ev_rg_wa, ev_rg_ba, ev_rg_wx, ev_rg_bx, ev_rg_lam,
              ev_gla_w_up, ev_gla_b_up, ev_gla_norm_g, ev_w_out,
              od_w_in, od_b_f, od_q_norm_g, od_k_norm_g, od_w_out,
              mlp_w1, mlp_b1, mlp_w2, mlp_b2):
    for layer in range(DEPTH):
        shift, scale, gate = ada_mod(c, ada_w[layer, 0], ada_b[layer, 0])
        u = x * (1.0 + scale) + shift
        if layer % 2 == 0:
            e = layer // 2
            y = even_mixer(u, ev_w_in[e], ev_conv_w[e], ev_conv_b[e], ev_rg_wa[e], ev_rg_ba[e],
                           ev_rg_wx[e], ev_rg_bx[e], ev_rg_lam[e], ev_gla_w_up[e], ev_gla_b_up[e],
                           ev_gla_norm_g[e], ev_w_out[e])
        else:
            o = layer // 2
            y = fox_attention(u, od_w_in[o], od_b_f[o], od_q_norm_g[o], od_k_norm_g[o], od_w_out[o])
        x = layer_norm(ALPHA * x + gate * y, ln_g[layer, 0], ln_b[layer, 0])
        shift, scale, gate = ada_mod(c, ada_w[layer, 1], ada_b[layer, 1])
        u = x * (1.0 + scale) + shift
        y = jnp.square(jax.nn.relu(u @ mlp_w1[layer] + mlp_b1[layer])) @ mlp_w2[layer] + mlp_b2[layer]
        x = layer_norm(ALPHA * x + gate * y, ln_g[layer, 1], ln_b[layer, 1])
    return x
```

```python
import functools

import numpy as np
import jax
import jax.numpy as jnp
from jax import lax
from jax.experimental import pallas as pl
from jax.experimental.pallas import tpu as pltpu

F32 = jnp.float32
BF16 = jnp.bfloat16

D_MODEL = 1024
DEPTH = 2
RG_WIDTH = D_MODEL // 2
RG_BLOCKS = 8
RG_BLOCK = RG_WIDTH // RG_BLOCKS
CONV_WIDTH = 4
RG_C = 8.0
GLA_HEADS = 4
GLA_VDIM = D_MODEL // 2
GLA_KDIM = GLA_VDIM // 2
GLA_HK = GLA_KDIM // GLA_HEADS
GLA_HV = GLA_VDIM // GLA_HEADS
GLA_LOWRANK = 16
GLA_GATE_TAU = 16.0
GLA_CHUNK = 64
EVEN_IN = 2 * RG_WIDTH + 2 * GLA_KDIM + 2 * GLA_VDIM + GLA_LOWRANK
FOX_HEADS = 16
FOX_HD = D_MODEL // FOX_HEADS
FOX_PAIRS = FOX_HEADS // 2
ODD_IN = 4 * D_MODEL + FOX_HEADS
D_FF = 4 * D_MODEL
ALPHA = (2 * DEPTH) ** 0.25
LN_EPS = 1e-5
RMS_EPS = 1e-6

LANES = 128
SUBLANES = 8
VMEM_LIMIT = 56 * 1024 * 1024

EVEN_IN_PAD = EVEN_IN + (LANES - GLA_LOWRANK)
ODD_IN_PAD = ODD_IN + (LANES - FOX_HEADS)
MASK_NEG = -1e30

F_PARTS = 3
F_SLOTS = 8


def _cparams(sem):
    return pltpu.CompilerParams(dimension_semantics=sem, vmem_limit_bytes=VMEM_LIMIT)


def _resident(shape):
    nd = len(shape)
    return pl.BlockSpec(shape, lambda *_: (0,) * nd)


def _sigmoid(x):
    return 1.0 / (1.0 + jnp.exp(-x))


def _log_sigmoid(x):
    return jnp.minimum(x, 0.0) - jnp.log1p(jnp.exp(-jnp.abs(x)))


def _gelu_tanh(x):
    c = np.sqrt(2.0 / np.pi).astype(np.float32)
    return x * (0.5 * (1.0 + jnp.tanh(c * (x + 0.044715 * (x * x * x)))))


def _bdot(a, b):
    return jnp.dot(a, b, preferred_element_type=F32)


def _split2(a):
    hi = a.astype(BF16)
    lo = (a - hi.astype(F32)).astype(BF16)
    return hi, lo


def _dot_split(a, w):
    ah, al = _split2(a)
    wh, wl = _split2(w)
    return _bdot(ah, wh) + (_bdot(ah, wl) + _bdot(al, wh))


def _layer_norm(z, g, b):
    mu = jnp.mean(z, axis=-1, keepdims=True)
    zc = z - mu
    var = jnp.mean(zc * zc, axis=-1, keepdims=True)
    return zc * lax.rsqrt(var + LN_EPS) * g + b


def _modulate(x, mod):
    shift = mod[:, 0:D_MODEL]
    scale = mod[:, D_MODEL:2 * D_MODEL]
    return x * (1.0 + scale) + shift


def _gate(mod):
    return 1.0 + mod[:, 2 * D_MODEL:3 * D_MODEL]


ADA_TN = 1024


def _ada_kernel(c_ref, w_ref, b_ref, o_ref):
    c = c_ref[...]
    s = c * _sigmoid(c)
    o_ref[...] = _dot_split(s, w_ref[...]) + b_ref[...]


def _ada_call(c, ada_w, ada_b):
    n_mod = ada_w.shape[0]
    bsz = c.shape[0]
    return pl.pallas_call(
        _ada_kernel,
        grid=(n_mod, 3 * D_MODEL // ADA_TN),
        in_specs=[
            pl.BlockSpec((bsz, D_MODEL), lambda i, j: (0, 0)),
            pl.BlockSpec((None, D_MODEL, ADA_TN), lambda i, j: (i, 0, j)),
            pl.BlockSpec((None, 1, ADA_TN), lambda i, j: (i, 0, j)),
        ],
        out_specs=pl.BlockSpec((None, bsz, ADA_TN), lambda i, j: (i, 0, j)),
        out_shape=jax.ShapeDtypeStruct((n_mod, bsz, 3 * D_MODEL), F32),
        compiler_params=_cparams(("arbitrary", "arbitrary")),
        name="ada",
    )(c, ada_w, ada_b)


PROJ_TM = 512
PROJ_TN = 512


def _inproj_kernel(x_ref, mod_ref, w_ref, o_ref):
    u = _modulate(x_ref[...], mod_ref[...]).astype(BF16)
    n_out = o_ref.shape[-1]
    for c0 in range(0, n_out, PROJ_TN):
        c1 = min(c0 + PROJ_TN, n_out)
        o_ref[:, c0:c1] = _bdot(u, w_ref[:, c0:c1])


def _inproj_call(x, mod, w):
    bsz, seq, _ = x.shape
    n_out = w.shape[1]
    return pl.pallas_call(
        _inproj_kernel,
        grid=(bsz, seq // PROJ_TM),
        in_specs=[
            pl.BlockSpec((None, PROJ_TM, D_MODEL), lambda b, t: (b, t, 0)),
            pl.BlockSpec((None, 1, 3 * D_MODEL), lambda b, t: (b, 0, 0)),
            _resident(w.shape),
        ],
        out_specs=pl.BlockSpec((None, PROJ_TM, n_out), lambda b, t: (b, t, 0)),
        out_shape=jax.ShapeDtypeStruct((bsz, seq, n_out), F32),
        compiler_params=_cparams(("arbitrary", "arbitrary")),
        name="inproj",
    )(x, mod, w)


EV_T = 512
EV_CHUNKS = EV_T // GLA_CHUNK
_XR, _YR = 0, RG_WIDTH
_Q = 2 * RG_WIDTH
_K = _Q + GLA_KDIM
_V = _K + GLA_KDIM
_G = _V + GLA_VDIM
_ZL = _G + GLA_VDIM


def _shift_rows(x, d, row):
    del row
    return pltpu.roll(x, d, 0)


def _even_mixer_kernel(p_ref, convw_ref, convb_ref, wg_ref, ba_ref, bx_ref, lam_ref,
                       wup_ref, bup_ref, gn_ref, o_ref,
                       tail_ref, hc_ref, st_ref, qd_ref, kd_ref, ke_ref, v_ref, bc_ref, oo_ref):
    t_idx = pl.program_id(1)

    @pl.when(t_idx == 0)
    def _():
        tail_ref[...] = jnp.zeros_like(tail_ref)
        hc_ref[...] = jnp.zeros_like(hc_ref)
        st_ref[...] = jnp.zeros_like(st_ref)

    xr = p_ref[:, _XR:_XR + RG_WIDTH]
    tail = tail_ref[...]
    row8 = lax.broadcasted_iota(jnp.int32, (SUBLANES, RG_WIDTH), 0)
    xc = convb_ref[...] + convw_ref[CONV_WIDTH - 1:CONV_WIDTH, :] * xr
    for j in range(1, CONV_WIDTH):
        xs = pltpu.roll(xr, j, 0)
        head = jnp.where(row8 < j, pltpu.roll(tail, j, 0), xs[0:SUBLANES])
        xs = jnp.concatenate([head, xs[SUBLANES:]], axis=0)
        xc = xc + convw_ref[CONV_WIDTH - 1 - j:CONV_WIDTH - j, :] * xs
    tail_ref[...] = xr[EV_T - SUBLANES:EV_T]

    half = RG_WIDTH // 2
    r_parts, i_parts = [], []
    for j in range(2):
        ri = _bdot(xc[:, j * half:(j + 1) * half].astype(BF16), wg_ref[j])
        r_parts.append(ri[:, :half])
        i_parts.append(ri[:, half:])
    r = _sigmoid(jnp.concatenate(r_parts, axis=1) + ba_ref[...])
    ig = _sigmoid(jnp.concatenate(i_parts, axis=1) + bx_ref[...])
    log_a = RG_C * r * _log_sigmoid(lam_ref[...])
    a = jnp.exp(log_a)
    u = jnp.sqrt(-jnp.tanh(log_a) * (a * a + 1.0)) * (ig * xc)

    row = lax.broadcasted_iota(jnp.int32, (EV_T, RG_WIDTH), 0)
    d = 1
    while d < EV_T:
        keep = row >= d
        u = jnp.where(keep, a * pltpu.roll(u, d, 0) + u, u)
        a = jnp.where(keep, a * pltpu.roll(a, d, 0), a)
        d *= 2
    h = u + a * hc_ref[SUBLANES - 1:SUBLANES, :]
    hc_ref[...] = h[EV_T - SUBLANES:EV_T]
    o_ref[:, 0:RG_WIDTH] = (h * _gelu_tanh(p_ref[:, _YR:_YR + RG_WIDTH])).astype(BF16)

    z = _dot_split(p_ref[:, _ZL:_ZL + LANES], wup_ref[...]) + bup_ref[...]
    bc = _log_sigmoid(z) * (1.0 / GLA_GATE_TAU)
    rowk = lax.broadcasted_iota(jnp.int32, (EV_T, GLA_KDIM), 0) & (GLA_CHUNK - 1)
    d = 1
    while d < GLA_CHUNK:
        bc = jnp.where(rowk >= d, bc + pltpu.roll(bc, d, 0), bc)
        d *= 2
    bc3 = bc.reshape(EV_CHUNKS, GLA_CHUNK, GLA_KDIM)
    b_last = jnp.broadcast_to(bc3[:, GLA_CHUNK - 1:GLA_CHUNK, :], bc3.shape).reshape(EV_T, GLA_KDIM)
    q = p_ref[:, _Q:_Q + GLA_KDIM]
    k = p_ref[:, _K:_K + GLA_KDIM]
    qd_ref[...] = (q * (GLA_HK ** -0.5) * jnp.exp(bc)).astype(BF16)
    kd_ref[...] = (k * jnp.exp(-bc)).astype(BF16)
    ke_ref[...] = (k * jnp.exp(b_last - bc)).astype(BF16)
    v_ref[...] = p_ref[:, _V:_V + GLA_VDIM].astype(BF16)
    bc_ref[...] = bc

    lane_k = lax.broadcasted_iota(jnp.int32, (GLA_CHUNK, GLA_KDIM), 1) // GLA_HK
    lane_v = lax.broadcasted_iota(jnp.int32, (GLA_CHUNK, GLA_VDIM), 1) // GLA_HV
    tri_r = lax.broadcasted_iota(jnp.int32, (GLA_CHUNK, GLA_HEADS * GLA_CHUNK), 0)
    tri_c = lax.broadcasted_iota(jnp.int32, (GLA_CHUNK, GLA_HEADS * GLA_CHUNK), 1) & (GLA_CHUNK - 1)
    st_r = lax.broadcasted_iota(jnp.int32, (GLA_VDIM, GLA_KDIM), 0) // GLA_HV
    st_c = lax.broadcasted_iota(jnp.int32, (GLA_VDIM, GLA_KDIM), 1) // GLA_HK
    nt = (((1,), (1,)), ((), ()))
    tn = (((0,), (0,)), ((), ()))

    def chunk_body(c, carry):
        r0 = pl.multiple_of(c * GLA_CHUNK, GLA_CHUNK)
        qd = qd_ref[pl.ds(r0, GLA_CHUNK), :]
        kd = kd_ref[pl.ds(r0, GLA_CHUNK), :]
        ke = ke_ref[pl.ds(r0, GLA_CHUNK), :]
        vv = v_ref[pl.ds(r0, GLA_CHUNK), :]
        zk = jnp.zeros_like(kd)
        zv = jnp.zeros_like(vv)
        kbd = jnp.concatenate([jnp.where(lane_k == hh, kd, zk) for hh in range(GLA_HEADS)], axis=0)
        att = lax.dot_general(qd, kbd, nt, preferred_element_type=F32)
        att = jnp.where(tri_c <= tri_r, att, 0.0).astype(BF16)
        vbd = jnp.concatenate([jnp.where(lane_v == hh, vv, zv) for hh in range(GLA_HEADS)], axis=0)
        st = st_ref[...]
        o = _bdot(att, vbd) + lax.dot_general(qd, st.astype(BF16), nt, preferred_element_type=F32)
        oo_ref[pl.ds(r0, GLA_CHUNK), :] = o
        kv_t = lax.dot_general(vv, ke, tn, preferred_element_type=F32)
        decay = jnp.exp(bc_ref[pl.ds(r0 + GLA_CHUNK - 1, 1), :])
        st_ref[...] = decay * st + jnp.where(st_r == st_c, kv_t, 0.0)
        return carry

    lax.fori_loop(0, EV_CHUNKS, chunk_body, 0)

    for hh in range(GLA_HEADS):
        oh = oo_ref[:, hh * GLA_HV:(hh + 1) * GLA_HV]
        ms = jnp.mean(oh * oh, axis=-1, keepdims=True)
        gg = p_ref[:, _G + hh * GLA_HV:_G + (hh + 1) * GLA_HV]
        on = oh * lax.rsqrt(ms + RMS_EPS) * gn_ref[:, hh * GLA_HV:(hh + 1) * GLA_HV]
        o_ref[:, RG_WIDTH + hh * GLA_HV:RG_WIDTH + (hh + 1) * GLA_HV] = (on * (gg * _sigmoid(gg))).astype(BF16)


def _even_mixer_call(proj, conv_w, conv_b, wg, b_a, b_x, lam, w_up, b_up, gn):
    bsz, seq, _ = proj.shape
    small = [conv_w, conv_b, wg, b_a, b_x, lam, w_up, b_up, gn]
    return pl.pallas_call(
        _even_mixer_kernel,
        grid=(bsz, seq // EV_T),
        in_specs=[pl.BlockSpec((None, EV_T, EVEN_IN_PAD), lambda b, t: (b, t, 0))]
        + [_resident(a.shape) for a in small],
        out_specs=pl.BlockSpec((None, EV_T, D_MODEL), lambda b, t: (b, t, 0)),
        out_shape=jax.ShapeDtypeStruct((bsz, seq, D_MODEL), BF16),
        scratch_shapes=[
            pltpu.VMEM((SUBLANES, RG_WIDTH), F32),
            pltpu.VMEM((SUBLANES, RG_WIDTH), F32),
            pltpu.VMEM((GLA_VDIM, GLA_KDIM), F32),
            pltpu.VMEM((EV_T, GLA_KDIM), BF16),
            pltpu.VMEM((EV_T, GLA_KDIM), BF16),
            pltpu.VMEM((EV_T, GLA_KDIM), BF16),
            pltpu.VMEM((EV_T, GLA_VDIM), BF16),
            pltpu.VMEM((EV_T, GLA_KDIM), F32),
            pltpu.VMEM((EV_T, GLA_VDIM), F32),
        ],
        compiler_params=_cparams(("arbitrary", "arbitrary")),
        name="even_mixer",
    )(proj, *small)


OUT_TM = 512


def _outproj_ln_kernel(a_ref, w_ref, x_ref, mod_ref, g_ref, b_ref, o_ref):
    y = _bdot(a_ref[...], w_ref[...])
    z = ALPHA * x_ref[...] + _gate(mod_ref[...]) * y
    o_ref[...] = _layer_norm(z, g_ref[...], b_ref[...])


def _outproj_ln_call(a, w, x, mod, ln_g, ln_b):
    bsz, seq, _ = x.shape
    return pl.pallas_call(
        _outproj_ln_kernel,
        grid=(bsz, seq // OUT_TM),
        in_specs=[
            pl.BlockSpec((None, OUT_TM, D_MODEL), lambda b, t: (b, t, 0)),
            _resident(w.shape),
            pl.BlockSpec((None, OUT_TM, D_MODEL), lambda b, t: (b, t, 0)),
            pl.BlockSpec((None, 1, 3 * D_MODEL), lambda b, t: (b, 0, 0)),
            _resident(ln_g.shape),
            _resident(ln_b.shape),
        ],
        out_specs=pl.BlockSpec((None, OUT_TM, D_MODEL), lambda b, t: (b, t, 0)),
        out_shape=jax.ShapeDtypeStruct((bsz, seq, D_MODEL), F32),
        compiler_params=_cparams(("arbitrary", "arbitrary")),
        name="outproj_ln",
    )(a, w, x, mod, ln_g, ln_b)


MLP_TM = 512
MLP_TF = 1024


def _mlp_kernel(x_ref, mod_ref, w1_ref, b1_ref, w2_ref, b2_ref, g_ref, b_ref, o_ref):
    x = x_ref[...]
    mod = mod_ref[...]
    u = _modulate(x, mod).astype(BF16)
    y = jnp.zeros((MLP_TM, D_MODEL), F32) + b2_ref[...]
    for f0 in range(0, D_FF, MLP_TF):
        hdn = _bdot(u, w1_ref[:, f0:f0 + MLP_TF]) + b1_ref[:, f0:f0 + MLP_TF]
        hdn = jnp.maximum(hdn, 0.0)
        y = y + _bdot((hdn * hdn).astype(BF16), w2_ref[f0:f0 + MLP_TF, :])
    z = ALPHA * x + _gate(mod) * y
    o_ref[...] = _layer_norm(z, g_ref[...], b_ref[...])


def _mlp_call(x, mod, w1, b1, w2, b2, ln_g, ln_b):
    bsz, seq, _ = x.shape
    return pl.pallas_call(
        _mlp_kernel,
        grid=(bsz, seq // MLP_TM),
        in_specs=[
            pl.BlockSpec((None, MLP_TM, D_MODEL), lambda b, t: (b, t, 0)),
            pl.BlockSpec((None, 1, 3 * D_MODEL), lambda b, t: (b, 0, 0)),
            _resident(w1.shape), _resident(b1.shape), _resident(w2.shape), _resident(b2.shape),
            _resident(ln_g.shape), _resident(ln_b.shape),
        ],
        out_specs=pl.BlockSpec((None, MLP_TM, D_MODEL), lambda b, t: (b, t, 0)),
        out_shape=jax.ShapeDtypeStruct((bsz, seq, D_MODEL), F32),
        compiler_params=_cparams(("arbitrary", "arbitrary")),
        name="mlp",
    )(x, mod, w1, b1, w2, b2, ln_g, ln_b)


FP_T = 512
_OQ, _OK, _OV, _OG, _OF = 0, D_MODEL, 2 * D_MODEL, 3 * D_MODEL, 4 * D_MODEL
F_ONE_COL = F_PARTS * FOX_HEADS


def _fox_place_matrices():
    pq = np.zeros((LANES, FOX_PAIRS * LANES), np.float32)
    pk = np.zeros((LANES, FOX_HEADS * LANES), np.float32)
    for h in range(FOX_HEADS):
        pair, e = divmod(h, 2)
        for part in range(F_PARTS):
            pq[part * FOX_HEADS + h, pair * LANES + F_SLOTS * e + part] = 1.0
            pq[F_ONE_COL, pair * LANES + F_SLOTS * e + F_PARTS + part] = 1.0
            pk[F_ONE_COL, h * LANES + F_SLOTS * e + part] = 1.0
            pk[part * FOX_HEADS + h, h * LANES + F_SLOTS * e + F_PARTS + part] = -1.0
    return pq, pk


def _fox_prep_kernel(p_ref, bf_ref, gq_ref, gk_ref, pq_ref, pk_ref, qa_ref, ka_ref, va_ref, fc_ref):
    t_idx = pl.program_id(1)

    @pl.when(t_idx == 0)
    def _():
        fc_ref[...] = jnp.zeros_like(fc_ref)

    lane = lax.broadcasted_iota(jnp.int32, (FP_T, LANES), 1)
    f = _log_sigmoid(p_ref[:, _OF:_OF + LANES] + bf_ref[...])
    row = lax.broadcasted_iota(jnp.int32, (FP_T, LANES), 0)
    d = 1
    while d < FP_T:
        f = jnp.where(row >= d, f + pltpu.roll(f, d, 0), f)
        d *= 2
    f = f + fc_ref[SUBLANES - 1:SUBLANES, :]
    fc_ref[...] = f[FP_T - SUBLANES:FP_T]
    f = jnp.where(lane < FOX_HEADS, f, 0.0)
    f1 = f.astype(BF16).astype(F32)
    r1 = f - f1
    f2 = r1.astype(BF16).astype(F32)
    f3 = (r1 - f2).astype(BF16).astype(F32)
    packed = f1 + pltpu.roll(f2, FOX_HEADS, 1) + pltpu.roll(f3, 2 * FOX_HEADS, 1)
    packed = jnp.where(lane == F_ONE_COL, 1.0, packed).astype(BF16)
    fq = _bdot(packed, pq_ref[...])
    fk = _bdot(packed, pk_ref[...])

    left = lane < FOX_HD

    def head_norm(xp, g):
        sq = xp * xp
        ms_l = jnp.sum(jnp.where(left, sq, 0.0), axis=-1, keepdims=True)
        ms_r = jnp.sum(jnp.where(left, 0.0, sq), axis=-1, keepdims=True)
        ms = jnp.where(left, ms_l, ms_r) * (1.0 / FOX_HD)
        return xp * lax.rsqrt(ms + RMS_EPS) * g

    for pair in range(FOX_PAIRS):
        c0 = pair * LANES
        qn = head_norm(p_ref[:, _OQ + c0:_OQ + c0 + LANES], gq_ref[...]) * (FOX_HD ** -0.5)
        kn = head_norm(p_ref[:, _OK + c0:_OK + c0 + LANES], gk_ref[...])
        vv = p_ref[:, _OV + c0:_OV + c0 + LANES]
        qa_ref[pair, :, 0:LANES] = qn.astype(BF16)
        qa_ref[pair, :, LANES:2 * LANES] = fq[:, c0:c0 + LANES].astype(BF16)
        for e in range(2):
            h = 2 * pair + e
            own = left if e == 0 else jnp.logical_not(left)
            ka_ref[h, :, 0:LANES] = jnp.where(own, kn, 0.0).astype(BF16)
            ka_ref[h, :, LANES:2 * LANES] = fk[:, h * LANES:(h + 1) * LANES].astype(BF16)
            va_ref[h] = jnp.where(own, vv, 1.0).astype(BF16)


def _fox_prep_call(proj, b_f, gq, gk, pq, pk):
    bsz, seq, _ = proj.shape
    small = [b_f, gq, gk, pq, pk]
    return pl.pallas_call(
        _fox_prep_kernel,
        grid=(bsz, seq // FP_T),
        in_specs=[pl.BlockSpec((None, FP_T, ODD_IN_PAD), lambda b, t: (b, t, 0))]
        + [_resident(a.shape) for a in small],
        out_specs=[
            pl.BlockSpec((None, FOX_PAIRS, FP_T, 2 * LANES), lambda b, t: (b, 0, t, 0)),
            pl.BlockSpec((None, FOX_HEADS, FP_T, 2 * LANES), lambda b, t: (b, 0, t, 0)),
            pl.BlockSpec((None, FOX_HEADS, FP_T, LANES), lambda b, t: (b, 0, t, 0)),
        ],
        out_shape=[
            jax.ShapeDtypeStruct((bsz, FOX_PAIRS, seq, 2 * LANES), BF16),
            jax.ShapeDtypeStruct((bsz, FOX_HEADS, seq, 2 * LANES), BF16),
            jax.ShapeDtypeStruct((bsz, FOX_HEADS, seq, LANES), BF16),
        ],
        scratch_shapes=[pltpu.VMEM((SUBLANES, LANES), F32)],
        compiler_params=_cparams(("arbitrary", "arbitrary")),
        name="fox_prep",
    )(proj, *small)


FA_T = 512


def _fox_attn_kernel(qa_ref, ka_ref, va_ref, g_ref, o_ref, m_ref, acc_ref):
    qi = pl.program_id(2)
    q = qa_ref[...]
    nt = (((1,), (1,)), ((), ()))
    row = lax.broadcasted_iota(jnp.int32, (FA_T, FA_T), 0)
    col = lax.broadcasted_iota(jnp.int32, (FA_T, FA_T), 1)
    left = lax.broadcasted_iota(jnp.int32, (FA_T, LANES), 1) < FOX_HD
    out = jnp.zeros((FA_T, LANES), F32)
    for e in range(2):
        m_ref[...] = jnp.full_like(m_ref, MASK_NEG)
        acc_ref[...] = jnp.zeros_like(acc_ref)

        def step(j, masked, e=e):
            k0 = pl.multiple_of(j * FA_T, FA_T)
            s = lax.dot_general(q, ka_ref[e, pl.ds(k0, FA_T), :], nt, preferred_element_type=F32)
            if masked:
                s = jnp.where(col <= row, s, MASK_NEG)
            m_prev = m_ref[...]
            m_new = jnp.maximum(m_prev, jnp.max(s, axis=-1, keepdims=True))
            p = jnp.exp(s - m_new).astype(BF16)
            acc_ref[...] = jnp.exp(m_prev - m_new) * acc_ref[...] + _bdot(p, va_ref[e, pl.ds(k0, FA_T), :])
            m_ref[...] = m_new

        def body(j, carry):
            step(j, False)
            return carry

        lax.fori_loop(0, qi, body, 0)
        step(qi, True)
        acc = acc_ref[...]
        o_e = acc / pltpu.roll(acc, FOX_HD, 1)
        own = left if e == 0 else jnp.logical_not(left)
        out = jnp.where(own, o_e, out)
    g = g_ref[...]
    o_ref[...] = (out * _sigmoid(g)).astype(BF16)


def _fox_attn_call(qa, ka, va, proj):
    bsz, _, seq, _ = qa.shape
    g_blk = _OG // LANES
    return pl.pallas_call(
        _fox_attn_kernel,
        grid=(bsz, FOX_PAIRS, seq // FA_T),
        in_specs=[
            pl.BlockSpec((None, None, FA_T, 2 * LANES), lambda b, p, t: (b, p, t, 0)),
            pl.BlockSpec((None, 2, seq, 2 * LANES), lambda b, p, t: (b, p, 0, 0)),
            pl.BlockSpec((None, 2, seq, LANES), lambda b, p, t: (b, p, 0, 0)),
            pl.BlockSpec((None, FA_T, LANES), lambda b, p, t: (b, t, g_blk + p)),
        ],
        out_specs=pl.BlockSpec((None, FA_T, LANES), lambda b, p, t: (b, t, p)),
        out_shape=jax.ShapeDtypeStruct((bsz, seq, D_MODEL), BF16),
        scratch_shapes=[
            pltpu.VMEM((FA_T, 1), F32),
            pltpu.VMEM((FA_T, LANES), F32),
        ],
        compiler_params=_cparams(("arbitrary", "arbitrary", "arbitrary")),
        name="fox_attn",
    )(qa, ka, va, proj)


def _pad_cols(w, n):
    return jnp.pad(w, ((0, 0), (0, n - w.shape[1])))


def _gate_weights(w_a, w_x):
    per_half = RG_BLOCKS // 2
    half = RG_WIDTH // 2

    def bd(w, j):
        m = jnp.zeros((half, half), F32)
        for i in range(per_half):
            m = lax.dynamic_update_slice(m, w[j * per_half + i], (i * RG_BLOCK, i * RG_BLOCK))
        return m

    return jnp.stack([jnp.concatenate([bd(w_a, j), bd(w_x, j)], axis=1) for j in range(2)]).astype(BF16)


def kernel(x, c, ada_w, ada_b, ln_g, ln_b, ev_w_in, ev_conv_w, ev_conv_b, ev_rg_wa, ev_rg_ba, ev_rg_wx,
           ev_rg_bx, ev_rg_lam, ev_gla_w_up, ev_gla_b_up, ev_gla_norm_g, ev_w_out, od_w_in, od_b_f,
           od_q_norm_g, od_k_norm_g, od_w_out, mlp_w1, mlp_b1, mlp_w2, mlp_b2):
    bsz = x.shape[0]
    mods = _ada_call(c, ada_w.reshape(2 * DEPTH, D_MODEL, 3 * D_MODEL),
                     ada_b.reshape(2 * DEPTH, 1, 3 * D_MODEL))
    mods = mods.reshape(2 * DEPTH, bsz, 1, 3 * D_MODEL)
    pq, pk = _fox_place_matrices()
    pq = jnp.asarray(pq, BF16)
    pk = jnp.asarray(pk, BF16)

    for layer in range(DEPTH):
        mod_mix = mods[2 * layer]
        mod_mlp = mods[2 * layer + 1]
        if layer % 2 == 0:
            e = layer // 2
            w_in = _pad_cols(ev_w_in[e], EVEN_IN_PAD).astype(BF16)
            proj = _inproj_call(x, mod_mix, w_in)
            w_up = jnp.pad(ev_gla_w_up[e], ((0, LANES - GLA_LOWRANK), (0, 0)))
            mix = _even_mixer_call(
                proj, ev_conv_w[e], ev_conv_b[e][None], _gate_weights(ev_rg_wa[e], ev_rg_wx[e]),
                ev_rg_ba[e][None], ev_rg_bx[e][None], ev_rg_lam[e][None], w_up, ev_gla_b_up[e][None],
                ev_gla_norm_g[e][None])
            w_out = ev_w_out[e].astype(BF16)
        else:
            o = layer // 2
            w_in = _pad_cols(od_w_in[o], ODD_IN_PAD).astype(BF16)
            proj = _inproj_call(x, mod_mix, w_in)
            b_f = jnp.pad(od_b_f[o], (0, LANES - FOX_HEADS))[None]
            gq = jnp.tile(od_q_norm_g[o], 2)[None]
            gk = jnp.tile(od_k_norm_g[o], 2)[None]
            qa, ka, va = _fox_prep_call(proj, b_f, gq, gk, pq, pk)
            mix = _fox_attn_call(qa, ka, va, proj)
            w_out = od_w_out[o].astype(BF16)
        x = _outproj_ln_call(mix, w_out, x, mod_mix, ln_g[layer, 0][None], ln_b[layer, 0][None])
        x = _mlp_call(x, mod_mlp, mlp_w1[layer].astype(BF16), mlp_b1[layer][None],
                      mlp_w2[layer].astype(BF16), mlp_b2[layer][None],
                      ln_g[layer, 1][None], ln_b[layer, 1][None])
    return x
```

```python
import functools

import numpy as np
import jax
import jax.numpy as jnp
from jax import lax
from jax.experimental import pallas as pl
from jax.experimental.pallas import tpu as pltpu

F32 = jnp.float32
BF16 = jnp.bfloat16

D_MODEL = 1024
DEPTH = 2
RG_WIDTH = D_MODEL // 2
RG_BLOCKS = 8
RG_BLOCK = RG_WIDTH // RG_BLOCKS
CONV_WIDTH = 4
RG_C = 8.0
GLA_HEADS = 4
GLA_VDIM = D_MODEL // 2
GLA_KDIM = GLA_VDIM // 2
GLA_HK = GLA_KDIM // GLA_HEADS
GLA_HV = GLA_VDIM // GLA_HEADS
GLA_LOWRANK = 16
GLA_GATE_TAU = 16.0
GLA_CHUNK = 64
EVEN_IN = 2 * RG_WIDTH + 2 * GLA_KDIM + 2 * GLA_VDIM + GLA_LOWRANK
FOX_HEADS = 16
FOX_HD = D_MODEL // FOX_HEADS
FOX_PAIRS = FOX_HEADS // 2
ODD_IN = 4 * D_MODEL + FOX_HEADS
D_FF = 4 * D_MODEL
ALPHA = (2 * DEPTH) ** 0.25
LN_EPS = 1e-5
RMS_EPS = 1e-6

LANES = 128
SUBLANES = 8
VMEM_LIMIT = 56 * 1024 * 1024

EVEN_IN_PAD = EVEN_IN + (LANES - GLA_LOWRANK)
ODD_IN_PAD = ODD_IN + (LANES - FOX_HEADS)
MASK_NEG = -1e30
LOG2E = float(np.log2(np.e))

F_PARTS = 3
F_SLOTS = 8


def _cparams(sem):
    return pltpu.CompilerParams(dimension_semantics=sem, vmem_limit_bytes=VMEM_LIMIT)


def _resident(shape):
    nd = len(shape)
    return pl.BlockSpec(shape, lambda *_: (0,) * nd)


def _sigmoid(x):
    return 1.0 / (1.0 + jnp.exp(-x))


def _log_sigmoid(x):
    return jnp.minimum(x, 0.0) - jnp.log1p(jnp.exp(-jnp.abs(x)))


def _gelu_tanh(x):
    c = np.sqrt(2.0 / np.pi).astype(np.float32)
    return x * (0.5 * (1.0 + jnp.tanh(c * (x + 0.044715 * (x * x * x)))))


def _bdot(a, b):
    return jnp.dot(a, b, preferred_element_type=F32)


def _split2(a):
    hi = a.astype(BF16)
    lo = (a - hi.astype(F32)).astype(BF16)
    return hi, lo


def _dot_split(a, w):
    ah, al = _split2(a)
    wh, wl = _split2(w)
    return _bdot(ah, wh) + (_bdot(ah, wl) + _bdot(al, wh))


def _layer_norm(z, g, b):
    mu = jnp.mean(z, axis=-1, keepdims=True)
    zc = z - mu
    var = jnp.mean(zc * zc, axis=-1, keepdims=True)
    return zc * lax.rsqrt(var + LN_EPS) * g + b


def _modulate(x, mod):
    shift = mod[:, 0:D_MODEL]
    scale = mod[:, D_MODEL:2 * D_MODEL]
    return x * (1.0 + scale) + shift


def _gate(mod):
    return 1.0 + mod[:, 2 * D_MODEL:3 * D_MODEL]


ADA_TN = 1024


def _ada_kernel(c_ref, w_ref, b_ref, o_ref):
    c = c_ref[...]
    s = c * _sigmoid(c)
    o_ref[...] = _dot_split(s, w_ref[...]) + b_ref[...]


def _ada_call(c, ada_w, ada_b):
    n_mod = ada_w.shape[0]
    bsz = c.shape[0]
    return pl.pallas_call(
        _ada_kernel,
        grid=(n_mod, 3 * D_MODEL // ADA_TN),
        in_specs=[
            pl.BlockSpec((bsz, D_MODEL), lambda i, j: (0, 0)),
            pl.BlockSpec((None, D_MODEL, ADA_TN), lambda i, j: (i, 0, j)),
            pl.BlockSpec((None, 1, ADA_TN), lambda i, j: (i, 0, j)),
        ],
        out_specs=pl.BlockSpec((None, bsz, ADA_TN), lambda i, j: (i, 0, j)),
        out_shape=jax.ShapeDtypeStruct((n_mod, bsz, 3 * D_MODEL), F32),
        compiler_params=_cparams(("arbitrary", "arbitrary")),
        name="ada",
    )(c, ada_w, ada_b)


PROJ_TM = 512
PROJ_TN = 512


def _inproj_kernel(x_ref, mod_ref, w_ref, o_ref):
    u = _modulate(x_ref[...], mod_ref[...]).astype(BF16)
    n_out = o_ref.shape[-1]
    for c0 in range(0, n_out, PROJ_TN):
        c1 = min(c0 + PROJ_TN, n_out)
        o_ref[:, c0:c1] = _bdot(u, w_ref[:, c0:c1])


def _inproj_call(x, mod, w):
    bsz, seq, _ = x.shape
    n_out = w.shape[1]
    return pl.pallas_call(
        _inproj_kernel,
        grid=(bsz, seq // PROJ_TM),
        in_specs=[
            pl.BlockSpec((None, PROJ_TM, D_MODEL), lambda b, t: (b, t, 0)),
            pl.BlockSpec((None, 1, 3 * D_MODEL), lambda b, t: (b, 0, 0)),
            _resident(w.shape),
        ],
        out_specs=pl.BlockSpec((None, PROJ_TM, n_out), lambda b, t: (b, t, 0)),
        out_shape=jax.ShapeDtypeStruct((bsz, seq, n_out), F32),
        compiler_params=_cparams(("arbitrary", "arbitrary")),
        name="inproj",
    )(x, mod, w)


EV_T = 512
EV_CHUNKS = EV_T // GLA_CHUNK
_XR, _YR = 0, RG_WIDTH
_Q = 2 * RG_WIDTH
_K = _Q + GLA_KDIM
_V = _K + GLA_KDIM
_G = _V + GLA_VDIM
_ZL = _G + GLA_VDIM


def _shift_rows(x, d, row):
    del row
    return pltpu.roll(x, d, 0)


def _even_mixer_kernel(p_ref, convw_ref, convb_ref, wg_ref, ba_ref, bx_ref, lam_ref,
                       wup_ref, bup_ref, gn_ref, o_ref,
                       tail_ref, hc_ref, st_ref, qd_ref, kd_ref, ke_ref, v_ref, bc_ref, oo_ref):
    t_idx = pl.program_id(1)

    @pl.when(t_idx == 0)
    def _():
        tail_ref[...] = jnp.zeros_like(tail_ref)
        hc_ref[...] = jnp.zeros_like(hc_ref)
        st_ref[...] = jnp.zeros_like(st_ref)

    xr = p_ref[:, _XR:_XR + RG_WIDTH]
    tail = tail_ref[...]
    row8 = lax.broadcasted_iota(jnp.int32, (SUBLANES, RG_WIDTH), 0)
    xc = convb_ref[...] + convw_ref[CONV_WIDTH - 1:CONV_WIDTH, :] * xr
    for j in range(1, CONV_WIDTH):
        xs = pltpu.roll(xr, j, 0)
        head = jnp.where(row8 < j, pltpu.roll(tail, j, 0), xs[0:SUBLANES])
        xs = jnp.concatenate([head, xs[SUBLANES:]], axis=0)
        xc = xc + convw_ref[CONV_WIDTH - 1 - j:CONV_WIDTH - j, :] * xs
    tail_ref[...] = xr[EV_T - SUBLANES:EV_T]

    half = RG_WIDTH // 2
    r_parts, i_parts = [], []
    for j in range(2):
        ri = _bdot(xc[:, j * half:(j + 1) * half].astype(BF16), wg_ref[j])
        r_parts.append(ri[:, :half])
        i_parts.append(ri[:, half:])
    r = _sigmoid(jnp.concatenate(r_parts, axis=1) + ba_ref[...])
    ig = _sigmoid(jnp.concatenate(i_parts, axis=1) + bx_ref[...])
    log_a = RG_C * r * _log_sigmoid(lam_ref[...])
    a = jnp.exp(log_a)
    u = jnp.sqrt(-jnp.tanh(log_a) * (a * a + 1.0)) * (ig * xc)

    row = lax.broadcasted_iota(jnp.int32, (EV_T, RG_WIDTH), 0)
    d = 1
    while d < EV_T:
        keep = row >= d
        u = jnp.where(keep, a * pltpu.roll(u, d, 0) + u, u)
        a = jnp.where(keep, a * pltpu.roll(a, d, 0), a)
        d *= 2
    h = u + a * hc_ref[SUBLANES - 1:SUBLANES, :]
    hc_ref[...] = h[EV_T - SUBLANES:EV_T]
    o_ref[:, 0:RG_WIDTH] = (h * _gelu_tanh(p_ref[:, _YR:_YR + RG_WIDTH])).astype(BF16)

    z = _dot_split(p_ref[:, _ZL:_ZL + LANES], wup_ref[...]) + bup_ref[...]
    bc = _log_sigmoid(z) * (1.0 / GLA_GATE_TAU)
    rowk = lax.broadcasted_iota(jnp.int32, (EV_T, GLA_KDIM), 0) & (GLA_CHUNK - 1)
    d = 1
    while d < GLA_CHUNK:
        bc = jnp.where(rowk >= d, bc + pltpu.roll(bc, d, 0), bc)
        d *= 2
    bc3 = bc.reshape(EV_CHUNKS, GLA_CHUNK, GLA_KDIM)
    b_last = jnp.broadcast_to(bc3[:, GLA_CHUNK - 1:GLA_CHUNK, :], bc3.shape).reshape(EV_T, GLA_KDIM)
    q = p_ref[:, _Q:_Q + GLA_KDIM]
    k = p_ref[:, _K:_K + GLA_KDIM]
    qd_ref[...] = (q * (GLA_HK ** -0.5) * jnp.exp(bc)).astype(BF16)
    kd_ref[...] = (k * jnp.exp(-bc)).astype(BF16)
    ke_ref[...] = (k * jnp.exp(b_last - bc)).astype(BF16)
    v_ref[...] = p_ref[:, _V:_V + GLA_VDIM].astype(BF16)
    bc_ref[...] = bc

    lane_k = lax.broadcasted_iota(jnp.int32, (GLA_CHUNK, GLA_KDIM), 1) // GLA_HK
    lane_v = lax.broadcasted_iota(jnp.int32, (GLA_CHUNK, GLA_VDIM), 1) // GLA_HV
    tri_r = lax.broadcasted_iota(jnp.int32, (GLA_CHUNK, GLA_HEADS * GLA_CHUNK), 0)
    tri_c = lax.broadcasted_iota(jnp.int32, (GLA_CHUNK, GLA_HEADS * GLA_CHUNK), 1) & (GLA_CHUNK - 1)
    st_r = lax.broadcasted_iota(jnp.int32, (GLA_VDIM, GLA_KDIM), 0) // GLA_HV
    st_c = lax.broadcasted_iota(jnp.int32, (GLA_VDIM, GLA_KDIM), 1) // GLA_HK
    nt = (((1,), (1,)), ((), ()))
    tn = (((0,), (0,)), ((), ()))

    def chunk_body(c, carry):
        r0 = pl.multiple_of(c * GLA_CHUNK, GLA_CHUNK)
        qd = qd_ref[pl.ds(r0, GLA_CHUNK), :]
        kd = kd_ref[pl.ds(r0, GLA_CHUNK), :]
        ke = ke_ref[pl.ds(r0, GLA_CHUNK), :]
        vv = v_ref[pl.ds(r0, GLA_CHUNK), :]
        zk = jnp.zeros_like(kd)
        zv = jnp.zeros_like(vv)
        kbd = jnp.concatenate([jnp.where(lane_k == hh, kd, zk) for hh in range(GLA_HEADS)], axis=0)
        att = lax.dot_general(qd, kbd, nt, preferred_element_type=F32)
        att = jnp.where(tri_c <= tri_r, att, 0.0).astype(BF16)
        vbd = jnp.concatenate([jnp.where(lane_v == hh, vv, zv) for hh in range(GLA_HEADS)], axis=0)
        st = st_ref[...]
        o = _bdot(att, vbd) + lax.dot_general(qd, st.astype(BF16), nt, preferred_element_type=F32)
        oo_ref[pl.ds(r0, GLA_CHUNK), :] = o
        kv_t = lax.dot_general(vv, ke, tn, preferred_element_type=F32)
        decay = jnp.exp(bc_ref[pl.ds(r0 + GLA_CHUNK - 1, 1), :])
        st_ref[...] = decay * st + jnp.where(st_r == st_c, kv_t, 0.0)
        return carry

    lax.fori_loop(0, EV_CHUNKS, chunk_body, 0)

    for hh in range(GLA_HEADS):
        oh = oo_ref[:, hh * GLA_HV:(hh + 1) * GLA_HV]
        ms = jnp.mean(oh * oh, axis=-1, keepdims=True)
        gg = p_ref[:, _G + hh * GLA_HV:_G + (hh + 1) * GLA_HV]
        on = oh * lax.rsqrt(ms + RMS_EPS) * gn_ref[:, hh * GLA_HV:(hh + 1) * GLA_HV]
        o_ref[:, RG_WIDTH + hh * GLA_HV:RG_WIDTH + (hh + 1) * GLA_HV] = (on * (gg * _sigmoid(gg))).astype(BF16)


def _even_mixer_call(proj, conv_w, conv_b, wg, b_a, b_x, lam, w_up, b_up, gn):
    bsz, seq, _ = proj.shape
    small = [conv_w, conv_b, wg, b_a, b_x, lam, w_up, b_up, gn]
    return pl.pallas_call(
        _even_mixer_kernel,
        grid=(bsz, seq // EV_T),
        in_specs=[pl.BlockSpec((None, EV_T, EVEN_IN_PAD), lambda b, t: (b, t, 0))]
        + [_resident(a.shape) for a in small],
        out_specs=pl.BlockSpec((None, EV_T, D_MODEL), lambda b, t: (b, t, 0)),
        out_shape=jax.ShapeDtypeStruct((bsz, seq, D_MODEL), BF16),
        scratch_shapes=[
            pltpu.VMEM((SUBLANES, RG_WIDTH), F32),
            pltpu.VMEM((SUBLANES, RG_WIDTH), F32),
            pltpu.VMEM((GLA_VDIM, GLA_KDIM), F32),
            pltpu.VMEM((EV_T, GLA_KDIM), BF16),
            pltpu.VMEM((EV_T, GLA_KDIM), BF16),
            pltpu.VMEM((EV_T, GLA_KDIM), BF16),
            pltpu.VMEM((EV_T, GLA_VDIM), BF16),
            pltpu.VMEM((EV_T, GLA_KDIM), F32),
            pltpu.VMEM((EV_T, GLA_VDIM), F32),
        ],
        compiler_params=_cparams(("arbitrary", "arbitrary")),
        name="even_mixer",
    )(proj, *small)


OUT_TM = 512


def _outproj_ln_kernel(a_ref, w_ref, x_ref, mod_ref, g_ref, b_ref, o_ref):
    y = _bdot(a_ref[...], w_ref[...])
    z = ALPHA * x_ref[...] + _gate(mod_ref[...]) * y
    o_ref[...] = _layer_norm(z, g_ref[...], b_ref[...])


def _outproj_ln_call(a, w, x, mod, ln_g, ln_b):
    bsz, seq, _ = x.shape
    return pl.pallas_call(
        _outproj_ln_kernel,
        grid=(bsz, seq // OUT_TM),
        in_specs=[
            pl.BlockSpec((None, OUT_TM, D_MODEL), lambda b, t: (b, t, 0)),
            _resident(w.shape),
            pl.BlockSpec((None, OUT_TM, D_MODEL), lambda b, t: (b, t, 0)),
            pl.BlockSpec((None, 1, 3 * D_MODEL), lambda b, t: (b, 0, 0)),
            _resident(ln_g.shape),
            _resident(ln_b.shape),
        ],
        out_specs=pl.BlockSpec((None, OUT_TM, D_MODEL), lambda b, t: (b, t, 0)),
        out_shape=jax.ShapeDtypeStruct((bsz, seq, D_MODEL), F32),
        compiler_params=_cparams(("arbitrary", "arbitrary")),
        name="outproj_ln",
    )(a, w, x, mod, ln_g, ln_b)


MLP_TM = 512
MLP_TF = 1024


def _mlp_kernel(x_ref, mod_ref, w1_ref, b1_ref, w2_ref, b2_ref, g_ref, b_ref, o_ref):
    x = x_ref[...]
    mod = mod_ref[...]
    u = _modulate(x, mod).astype(BF16)
    y = jnp.zeros((MLP_TM, D_MODEL), F32) + b2_ref[...]
    for f0 in range(0, D_FF, MLP_TF):
        hdn = _bdot(u, w1_ref[:, f0:f0 + MLP_TF]) + b1_ref[:, f0:f0 + MLP_TF]
        hdn = jnp.maximum(hdn, 0.0)
        y = y + _bdot((hdn * hdn).astype(BF16), w2_ref[f0:f0 + MLP_TF, :])
    z = ALPHA * x + _gate(mod) * y
    o_ref[...] = _layer_norm(z, g_ref[...], b_ref[...])


def _mlp_call(x, mod, w1, b1, w2, b2, ln_g, ln_b):
    bsz, seq, _ = x.shape
    return pl.pallas_call(
        _mlp_kernel,
        grid=(bsz, seq // MLP_TM),
        in_specs=[
            pl.BlockSpec((None, MLP_TM, D_MODEL), lambda b, t: (b, t, 0)),
            pl.BlockSpec((None, 1, 3 * D_MODEL), lambda b, t: (b, 0, 0)),
            _resident(w1.shape), _resident(b1.shape), _resident(w2.shape), _resident(b2.shape),
            _resident(ln_g.shape), _resident(ln_b.shape),
        ],
        out_specs=pl.BlockSpec((None, MLP_TM, D_MODEL), lambda b, t: (b, t, 0)),
        out_shape=jax.ShapeDtypeStruct((bsz, seq, D_MODEL), F32),
        compiler_params=_cparams(("arbitrary", "arbitrary")),
        name="mlp",
    )(x, mod, w1, b1, w2, b2, ln_g, ln_b)


FP_T = 512
_OQ, _OK, _OV, _OG, _OF = 0, D_MODEL, 2 * D_MODEL, 3 * D_MODEL, 4 * D_MODEL
F_ONE_COL = F_PARTS * FOX_HEADS


def _fox_place_matrices():
    pq = np.zeros((LANES, FOX_PAIRS * LANES), np.float32)
    pk = np.zeros((LANES, FOX_HEADS * LANES), np.float32)
    for h in range(FOX_HEADS):
        pair, e = divmod(h, 2)
        for part in range(F_PARTS):
            pq[part * FOX_HEADS + h, pair * LANES + F_SLOTS * e + part] = 1.0
            pq[F_ONE_COL, pair * LANES + F_SLOTS * e + F_PARTS + part] = 1.0
            pk[F_ONE_COL, h * LANES + F_SLOTS * e + part] = 1.0
            pk[part * FOX_HEADS + h, h * LANES + F_SLOTS * e + F_PARTS + part] = -1.0
    return pq, pk


def _fox_prep_kernel(p_ref, bf_ref, gq_ref, gk_ref, pq_ref, pk_ref, qa_ref, ka_ref, vt_ref, fc_ref):
    t_idx = pl.program_id(1)

    @pl.when(t_idx == 0)
    def _():
        fc_ref[...] = jnp.zeros_like(fc_ref)

    lane = lax.broadcasted_iota(jnp.int32, (FP_T, LANES), 1)
    f = _log_sigmoid(p_ref[:, _OF:_OF + LANES] + bf_ref[...])
    row = lax.broadcasted_iota(jnp.int32, (FP_T, LANES), 0)
    d = 1
    while d < FP_T:
        f = jnp.where(row >= d, f + pltpu.roll(f, d, 0), f)
        d *= 2
    f = f + fc_ref[SUBLANES - 1:SUBLANES, :]
    fc_ref[...] = f[FP_T - SUBLANES:FP_T]
    f = jnp.where(lane < FOX_HEADS, f * LOG2E, 0.0)
    f1 = f.astype(BF16).astype(F32)
    r1 = f - f1
    f2 = r1.astype(BF16).astype(F32)
    f3 = (r1 - f2).astype(BF16).astype(F32)
    packed = f1 + pltpu.roll(f2, FOX_HEADS, 1) + pltpu.roll(f3, 2 * FOX_HEADS, 1)
    packed = jnp.where(lane == F_ONE_COL, 1.0, packed).astype(BF16)
    fq = _bdot(packed, pq_ref[...])
    fk = _bdot(packed, pk_ref[...])

    left = lane < FOX_HD

    def head_norm(xp, g):
        sq = xp * xp
        ms_l = jnp.sum(jnp.where(left, sq, 0.0), axis=-1, keepdims=True)
        ms_r = jnp.sum(jnp.where(left, 0.0, sq), axis=-1, keepdims=True)
        ms = jnp.where(left, ms_l, ms_r) * (1.0 / FOX_HD)
        return xp * lax.rsqrt(ms + RMS_EPS) * g

    for pair in range(FOX_PAIRS):
        c0 = pair * LANES
        qn = head_norm(p_ref[:, _OQ + c0:_OQ + c0 + LANES], gq_ref[...]) * (FOX_HD ** -0.5 * LOG2E)
        kn = head_norm(p_ref[:, _OK + c0:_OK + c0 + LANES], gk_ref[...])
        vv = p_ref[:, _OV + c0:_OV + c0 + LANES]
        qa_ref[pair, :, 0:LANES] = qn.astype(BF16)
        qa_ref[pair, :, LANES:2 * LANES] = fq[:, c0:c0 + LANES].astype(BF16)
        for e in range(2):
            h = 2 * pair + e
            own = left if e == 0 else jnp.logical_not(left)
            ka_ref[h, :, 0:LANES] = jnp.where(own, kn, 0.0).astype(BF16)
            ka_ref[h, :, LANES:2 * LANES] = fk[:, h * LANES:(h + 1) * LANES].astype(BF16)
            vt_ref[h, 0] = jnp.where(own, vv, 1.0).T.astype(BF16)


def _fox_prep_call(proj, b_f, gq, gk, pq, pk):
    bsz, seq, _ = proj.shape
    small = [b_f, gq, gk, pq, pk]
    return pl.pallas_call(
        _fox_prep_kernel,
        grid=(bsz, seq // FP_T),
        in_specs=[pl.BlockSpec((None, FP_T, ODD_IN_PAD), lambda b, t: (b, t, 0))]
        + [_resident(a.shape) for a in small],
        out_specs=[
            pl.BlockSpec((None, FOX_PAIRS, FP_T, 2 * LANES), lambda b, t: (b, 0, t, 0)),
            pl.BlockSpec((None, FOX_HEADS, FP_T, 2 * LANES), lambda b, t: (b, 0, t, 0)),
            pl.BlockSpec((None, FOX_HEADS, 1, LANES, FP_T), lambda b, t: (b, 0, t, 0, 0)),
        ],
        out_shape=[
            jax.ShapeDtypeStruct((bsz, FOX_PAIRS, seq, 2 * LANES), BF16),
            jax.ShapeDtypeStruct((bsz, FOX_HEADS, seq, 2 * LANES), BF16),
            jax.ShapeDtypeStruct((bsz, FOX_HEADS, seq // FP_T, LANES, FP_T), BF16),
        ],
        scratch_shapes=[pltpu.VMEM((SUBLANES, LANES), F32)],
        compiler_params=_cparams(("arbitrary", "arbitrary")),
        name="fox_prep",
    )(proj, *small)


FA_T = FP_T
FA_NH = 4
FA_NP = FA_NH // 2


def _fox_attn_kernel(qa_ref, ka_ref, vt_ref, g_ref, o_ref, m_ref, acc_ref, sa_ref, sb_ref):
    qi = pl.program_id(2)
    nt = (((1,), (1,)), ((), ()))
    key = lax.broadcasted_iota(jnp.int32, (FA_T, FA_T), 0)
    qry = lax.broadcasted_iota(jnp.int32, (FA_T, FA_T), 1)
    m_ref[...] = jnp.full_like(m_ref, MASK_NEG)
    acc_ref[...] = jnp.zeros_like(acc_ref)

    def scores(j, dst_ref, h):
        k0 = pl.multiple_of(j * FA_T, FA_T)
        dst_ref[h] = lax.dot_general(ka_ref[h, pl.ds(k0, FA_T), :], qa_ref[h // 2], nt,
                                     preferred_element_type=F32)

    def softmax_pv(j, src_ref, h, masked):
        s_t = src_ref[h]
        if masked:
            s_t = jnp.where(key <= qry, s_t, MASK_NEG)
        m_prev = m_ref[h]
        m_new = jnp.maximum(m_prev, jnp.max(s_t, axis=0, keepdims=True))
        p_t = jnp.exp2(s_t - m_new).astype(BF16)
        acc_ref[h] = jnp.exp2(m_prev - m_new) * acc_ref[h] + _bdot(vt_ref[h, j], p_t)
        m_ref[h] = m_new

    def pipelined_step(j, cur_ref, nxt_ref):
        scores(j + 1, nxt_ref, 0)
        for h in range(FA_NH):
            if h + 1 < FA_NH:
                scores(j + 1, nxt_ref, h + 1)
            softmax_pv(j, cur_ref, h, False)

    def masked_step(j, cur_ref):
        for h in range(FA_NH):
            softmax_pv(j, cur_ref, h, True)

    for h in range(FA_NH):
        scores(0, sa_ref, h)

    def pair_body(i, carry):
        pipelined_step(2 * i, sa_ref, sb_ref)
        pipelined_step(2 * i + 1, sb_ref, sa_ref)
        return carry

    lax.fori_loop(0, qi // 2, pair_body, 0)

    @pl.when(qi % 2 == 1)
    def _():
        pipelined_step(qi - 1, sa_ref, sb_ref)
        masked_step(qi, sb_ref)

    @pl.when(qi % 2 == 0)
    def _():
        masked_step(qi, sa_ref)
    top = lax.broadcasted_iota(jnp.int32, (LANES, FA_T), 0) < FOX_HD
    for p in range(FA_NP):
        acc0 = acc_ref[2 * p]
        acc1 = acc_ref[2 * p + 1]
        o_t = jnp.where(top, acc0 / acc0[FOX_HD:FOX_HD + 1, :], acc1 / acc1[0:1, :])
        g = g_ref[:, p * LANES:(p + 1) * LANES]
        o_ref[:, p * LANES:(p + 1) * LANES] = (o_t.T * _sigmoid(g)).astype(BF16)


def _fox_attn_call(qa, ka, vt, proj):
    bsz, _, seq, _ = qa.shape
    gw = FA_NP * LANES
    g_blk = _OG // gw
    return pl.pallas_call(
        _fox_attn_kernel,
        grid=(bsz, FOX_HEADS // FA_NH, seq // FA_T),
        in_specs=[
            pl.BlockSpec((None, FA_NP, FA_T, 2 * LANES), lambda b, p, t: (b, p, t, 0)),
            pl.BlockSpec((None, FA_NH, seq, 2 * LANES), lambda b, p, t: (b, p, 0, 0)),
            pl.BlockSpec((None, FA_NH, seq // FA_T, LANES, FA_T), lambda b, p, t: (b, p, 0, 0, 0)),
            pl.BlockSpec((None, FA_T, gw), lambda b, p, t: (b, t, g_blk + p)),
        ],
        out_specs=pl.BlockSpec((None, FA_T, gw), lambda b, p, t: (b, t, p)),
        out_shape=jax.ShapeDtypeStruct((bsz, seq, D_MODEL), BF16),
        scratch_shapes=[
            pltpu.VMEM((FA_NH, 1, FA_T), F32),
            pltpu.VMEM((FA_NH, LANES, FA_T), F32),
            pltpu.VMEM((FA_NH, FA_T, FA_T), F32),
            pltpu.VMEM((FA_NH, FA_T, FA_T), F32),
        ],
        compiler_params=_cparams(("arbitrary", "arbitrary", "arbitrary")),
        name="fox_attn",
    )(qa, ka, vt, proj)


def _pad_cols(w, n):
    return jnp.pad(w, ((0, 0), (0, n - w.shape[1])))


def _gate_weights(w_a, w_x):
    per_half = RG_BLOCKS // 2
    half = RG_WIDTH // 2

    def bd(w, j):
        m = jnp.zeros((half, half), F32)
        for i in range(per_half):
            m = lax.dynamic_update_slice(m, w[j * per_half + i], (i * RG_BLOCK, i * RG_BLOCK))
        return m

    return jnp.stack([jnp.concatenate([bd(w_a, j), bd(w_x, j)], axis=1) for j in range(2)]).astype(BF16)


def kernel(x, c, ada_w, ada_b, ln_g, ln_b, ev_w_in, ev_conv_w, ev_conv_b, ev_rg_wa, ev_rg_ba, ev_rg_wx,
           ev_rg_bx, ev_rg_lam, ev_gla_w_up, ev_gla_b_up, ev_gla_norm_g, ev_w_out, od_w_in, od_b_f,
           od_q_norm_g, od_k_norm_g, od_w_out, mlp_w1, mlp_b1, mlp_w2, mlp_b2):
    bsz = x.shape[0]
    mods = _ada_call(c, ada_w.reshape(2 * DEPTH, D_MODEL, 3 * D_MODEL),
                     ada_b.reshape(2 * DEPTH, 1, 3 * D_MODEL))
    mods = mods.reshape(2 * DEPTH, bsz, 1, 3 * D_MODEL)
    pq, pk = _fox_place_matrices()
    pq = jnp.asarray(pq, BF16)
    pk = jnp.asarray(pk, BF16)

    for layer in range(DEPTH):
        mod_mix = mods[2 * layer]
        mod_mlp = mods[2 * layer + 1]
        if layer % 2 == 0:
            e = layer // 2
            w_in = _pad_cols(ev_w_in[e], EVEN_IN_PAD).astype(BF16)
            proj = _inproj_call(x, mod_mix, w_in)
            w_up = jnp.pad(ev_gla_w_up[e], ((0, LANES - GLA_LOWRANK), (0, 0)))
            mix = _even_mixer_call(
                proj, ev_conv_w[e], ev_conv_b[e][None], _gate_weights(ev_rg_wa[e], ev_rg_wx[e]),
                ev_rg_ba[e][None], ev_rg_bx[e][None], ev_rg_lam[e][None], w_up, ev_gla_b_up[e][None],
                ev_gla_norm_g[e][None])
            w_out = ev_w_out[e].astype(BF16)
        else:
            o = layer // 2
            w_in = _pad_cols(od_w_in[o], ODD_IN_PAD).astype(BF16)
            proj = _inproj_call(x, mod_mix, w_in)
            b_f = jnp.pad(od_b_f[o], (0, LANES - FOX_HEADS))[None]
            gq = jnp.tile(od_q_norm_g[o], 2)[None]
            gk = jnp.tile(od_k_norm_g[o], 2)[None]
            qa, ka, vt = _fox_prep_call(proj, b_f, gq, gk, pq, pk)
            mix = _fox_attn_call(qa, ka, vt, proj)
            w_out = od_w_out[o].astype(BF16)
        x = _outproj_ln_call(mix, w_out, x, mod_mix, ln_g[layer, 0][None], ln_b[layer, 0][None])
        x = _mlp_call(x, mod_mlp, mlp_w1[layer].astype(BF16), mlp_b1[layer][None],
                      mlp_w2[layer].astype(BF16), mlp_b2[layer][None],
                      ln_g[layer, 1][None], ln_b[layer, 1][None])
    return x
```

```python
import numpy as np
import jax
import jax.numpy as jnp
from jax import lax
from jax.experimental import pallas as pl
from jax.experimental.pallas import tpu as pltpu

F32 = jnp.float32
BF16 = jnp.bfloat16

D_MODEL = 1024
DEPTH = 2
RG_WIDTH = D_MODEL // 2
RG_BLOCKS = 8
RG_BLOCK = RG_WIDTH // RG_BLOCKS
CONV_WIDTH = 4
RG_C = 8.0
GLA_HEADS = 4
GLA_VDIM = D_MODEL // 2
GLA_KDIM = GLA_VDIM // 2
GLA_HK = GLA_KDIM // GLA_HEADS
GLA_HV = GLA_VDIM // GLA_HEADS
GLA_LOWRANK = 16
GLA_GATE_TAU = 16.0
GLA_CHUNK = 64
EVEN_IN = 2 * RG_WIDTH + 2 * GLA_KDIM + 2 * GLA_VDIM + GLA_LOWRANK
FOX_HEADS = 16
FOX_HD = D_MODEL // FOX_HEADS
FOX_PAIRS = FOX_HEADS // 2
ODD_IN = 4 * D_MODEL + FOX_HEADS
D_FF = 4 * D_MODEL
ALPHA = (2 * DEPTH) ** 0.25
LN_EPS = 1e-5
RMS_EPS = 1e-6

LANES = 128
SUBLANES = 8
VMEM_LIMIT = 56 * 1024 * 1024

EVEN_IN_PAD = EVEN_IN + (LANES - GLA_LOWRANK)
ODD_IN_PAD = ODD_IN + (LANES - FOX_HEADS)
MASK_NEG = -1e30
LOG2E = float(np.log2(np.e))

F_PARTS = 3
F_SLOTS = 8

PROJ_TN = 512


def _cparams(sem):
    return pltpu.CompilerParams(dimension_semantics=sem, vmem_limit_bytes=VMEM_LIMIT)


def _resident(shape):
    nd = len(shape)
    return pl.BlockSpec(shape, lambda *_: (0,) * nd)


def _sigmoid(x):
    return 0.5 * jnp.tanh(0.5 * x) + 0.5


def _log_sigmoid(x):
    return jnp.minimum(x, 0.0) - jnp.log1p(jnp.exp(-jnp.abs(x)))


def _gelu_tanh(x):
    c = np.sqrt(2.0 / np.pi).astype(np.float32)
    return x * (0.5 * (1.0 + jnp.tanh(c * (x + 0.044715 * (x * x * x)))))


def _sqrt_nonneg(y):
    return jnp.where(y > 0.0, y * lax.rsqrt(y), 0.0)


def _bdot(a, b):
    return jnp.dot(a, b, preferred_element_type=F32)


def _split2(a):
    hi = a.astype(BF16)
    lo = (a - hi.astype(F32)).astype(BF16)
    return hi, lo


def _dot_split(a, w):
    ah, al = _split2(a)
    wh, wl = _split2(w)
    return _bdot(ah, wh) + (_bdot(ah, wl) + _bdot(al, wh))


def _layer_norm(z, g, b):
    mu = jnp.mean(z, axis=-1, keepdims=True)
    zc = z - mu
    var = jnp.mean(zc * zc, axis=-1, keepdims=True)
    return zc * lax.rsqrt(var + LN_EPS) * g + b


def _modulate(x, mod):
    shift = mod[:, 0:D_MODEL]
    scale = mod[:, D_MODEL:2 * D_MODEL]
    return x * (1.0 + scale) + shift


def _gate(mod):
    return 1.0 + mod[:, 2 * D_MODEL:3 * D_MODEL]


ADA_TN = 1024


def _ada_kernel(c_ref, w_ref, b_ref, o_ref):
    c = c_ref[...]
    s = c * _sigmoid(c)
    o_ref[...] = _dot_split(s, w_ref[...]) + b_ref[...]


def _ada_call(c, ada_w, ada_b):
    n_mod = ada_w.shape[0]
    bsz = c.shape[0]
    return pl.pallas_call(
        _ada_kernel,
        grid=(n_mod, 3 * D_MODEL // ADA_TN),
        in_specs=[
            pl.BlockSpec((bsz, D_MODEL), lambda i, j: (0, 0)),
            pl.BlockSpec((None, D_MODEL, ADA_TN), lambda i, j: (i, 0, j)),
            pl.BlockSpec((None, 1, ADA_TN), lambda i, j: (i, 0, j)),
        ],
        out_specs=pl.BlockSpec((None, bsz, ADA_TN), lambda i, j: (i, 0, j)),
        out_shape=jax.ShapeDtypeStruct((n_mod, bsz, 3 * D_MODEL), F32),
        compiler_params=_cparams(("arbitrary", "arbitrary")),
        name="ada",
    )(c, ada_w, ada_b)


EV_T = 512
EV_CHUNKS = EV_T // GLA_CHUNK
_XR, _YR = 0, RG_WIDTH
_Q = 2 * RG_WIDTH
_K = _Q + GLA_KDIM
_V = _K + GLA_KDIM
_G = _V + GLA_VDIM
_ZL = _G + GLA_VDIM


def _scan_rows8(a, u):
    rows, ch = a.shape
    a3 = a.reshape(rows // SUBLANES, SUBLANES, ch)
    u3 = u.reshape(rows // SUBLANES, SUBLANES, ch)
    sub = lax.broadcasted_iota(jnp.int32, a3.shape, 1)
    d = 1
    while d < SUBLANES:
        keep = sub >= d
        u3 = jnp.where(keep, a3 * pltpu.roll(u3, d, 1) + u3, u3)
        a3 = jnp.where(keep, a3 * pltpu.roll(a3, d, 1), a3)
        d *= 2
    return a3, u3


def _even_layer_kernel(x_ref, mod_ref, win_ref, convw_ref, convb_ref, wg_ref, ba_ref, bx_ref, lam_ref,
                       wup_ref, bup_ref, gn_ref, wout_ref, lng_ref, lnb_ref, o_ref,
                       p_ref, mix_ref, tail_ref, hc_ref, st_ref, qd_ref, kd_ref, ke_ref, v_ref, bc_ref,
                       oo_ref):
    t_idx = pl.program_id(1)

    @pl.when(t_idx == 0)
    def _():
        tail_ref[...] = jnp.zeros_like(tail_ref)
        hc_ref[...] = jnp.zeros_like(hc_ref)
        st_ref[...] = jnp.zeros_like(st_ref)

    u_in = _modulate(x_ref[...], mod_ref[...]).astype(BF16)
    for c0 in range(0, EVEN_IN_PAD, PROJ_TN):
        c1 = min(c0 + PROJ_TN, EVEN_IN_PAD)
        p_ref[:, c0:c1] = _bdot(u_in, win_ref[:, c0:c1])

    xr = p_ref[:, _XR:_XR + RG_WIDTH]
    tail = tail_ref[...]
    row8 = lax.broadcasted_iota(jnp.int32, (SUBLANES, RG_WIDTH), 0)
    xc = convb_ref[...] + convw_ref[CONV_WIDTH - 1:CONV_WIDTH, :] * xr
    for j in range(1, CONV_WIDTH):
        xs = pltpu.roll(xr, j, 0)
        head = jnp.where(row8 < j, pltpu.roll(tail, j, 0), xs[0:SUBLANES])
        xs = jnp.concatenate([head, xs[SUBLANES:]], axis=0)
        xc = xc + convw_ref[CONV_WIDTH - 1 - j:CONV_WIDTH - j, :] * xs
    tail_ref[...] = xr[EV_T - SUBLANES:EV_T]

    half = RG_WIDTH // 2
    r_parts, i_parts = [], []
    for j in range(2):
        ri = _bdot(xc[:, j * half:(j + 1) * half].astype(BF16), wg_ref[j])
        r_parts.append(ri[:, :half])
        i_parts.append(ri[:, half:])
    r = _sigmoid(jnp.concatenate(r_parts, axis=1) + ba_ref[...])
    ig = _sigmoid(jnp.concatenate(i_parts, axis=1) + bx_ref[...])
    log_a = RG_C * r * _log_sigmoid(lam_ref[...])
    a = jnp.exp(log_a)
    u = _sqrt_nonneg(-jnp.tanh(log_a) * (a * a + 1.0)) * (ig * xc)

    a3, h3 = _scan_rows8(a, u)
    carry = jnp.broadcast_to(hc_ref[SUBLANES - 1:SUBLANES, :], (SUBLANES, RG_WIDTH))
    h_groups = []
    for g in range(EV_T // SUBLANES):
        hg = h3[g] + a3[g] * carry
        h_groups.append(hg)
        carry = jnp.broadcast_to(hg[SUBLANES - 1:SUBLANES, :], (SUBLANES, RG_WIDTH))
    hc_ref[...] = carry
    h = jnp.concatenate(h_groups, axis=0)
    mix_ref[:, 0:RG_WIDTH] = (h * _gelu_tanh(p_ref[:, _YR:_YR + RG_WIDTH])).astype(BF16)

    z = _dot_split(p_ref[:, _ZL:_ZL + LANES], wup_ref[...]) + bup_ref[...]
    bc = _log_sigmoid(z) * (1.0 / GLA_GATE_TAU)
    rowk = lax.broadcasted_iota(jnp.int32, (EV_T, GLA_KDIM), 0) & (GLA_CHUNK - 1)
    d = 1
    while d < GLA_CHUNK:
        bc = jnp.where(rowk >= d, bc + pltpu.roll(bc, d, 0), bc)
        d *= 2
    bc3 = bc.reshape(EV_CHUNKS, GLA_CHUNK, GLA_KDIM)
    b_last = jnp.broadcast_to(bc3[:, GLA_CHUNK - 1:GLA_CHUNK, :], bc3.shape).reshape(EV_T, GLA_KDIM)
    q = p_ref[:, _Q:_Q + GLA_KDIM]
    k = p_ref[:, _K:_K + GLA_KDIM]
    qd_ref[...] = (q * (GLA_HK ** -0.5) * jnp.exp(bc)).astype(BF16)
    kd_ref[...] = (k * jnp.exp(-bc)).astype(BF16)
    ke_ref[...] = (k * jnp.exp(b_last - bc)).astype(BF16)
    v_ref[...] = p_ref[:, _V:_V + GLA_VDIM].astype(BF16)
    bc_ref[...] = bc

    lane_k = lax.broadcasted_iota(jnp.int32, (GLA_CHUNK, GLA_KDIM), 1) // GLA_HK
    lane_v = lax.broadcasted_iota(jnp.int32, (GLA_CHUNK, GLA_VDIM), 1) // GLA_HV
    tri_r = lax.broadcasted_iota(jnp.int32, (GLA_CHUNK, GLA_HEADS * GLA_CHUNK), 0)
    tri_c = lax.broadcasted_iota(jnp.int32, (GLA_CHUNK, GLA_HEADS * GLA_CHUNK), 1) & (GLA_CHUNK - 1)
    st_r = lax.broadcasted_iota(jnp.int32, (GLA_VDIM, GLA_KDIM), 0) // GLA_HV
    st_c = lax.broadcasted_iota(jnp.int32, (GLA_VDIM, GLA_KDIM), 1) // GLA_HK
    nt = (((1,), (1,)), ((), ()))
    tn = (((0,), (0,)), ((), ()))

    def chunk_body(c, carry):
        r0 = pl.multiple_of(c * GLA_CHUNK, GLA_CHUNK)
        qd = qd_ref[pl.ds(r0, GLA_CHUNK), :]
        kd = kd_ref[pl.ds(r0, GLA_CHUNK), :]
        ke = ke_ref[pl.ds(r0, GLA_CHUNK), :]
        vv = v_ref[pl.ds(r0, GLA_CHUNK), :]
        zk = jnp.zeros_like(kd)
        zv = jnp.zeros_like(vv)
        kbd = jnp.concatenate([jnp.where(lane_k == hh, kd, zk) for hh in range(GLA_HEADS)], axis=0)
        att = lax.dot_general(qd, kbd, nt, preferred_element_type=F32)
        att = jnp.where(tri_c <= tri_r, att, 0.0).astype(BF16)
        vbd = jnp.concatenate([jnp.where(lane_v == hh, vv, zv) for hh in range(GLA_HEADS)], axis=0)
        st = st_ref[...]
        o = _bdot(att, vbd) + lax.dot_general(qd, st.astype(BF16), nt, preferred_element_type=F32)
        oo_ref[pl.ds(r0, GLA_CHUNK), :] = o
        kv_t = lax.dot_general(vv, ke, tn, preferred_element_type=F32)
        decay = jnp.exp(bc_ref[pl.ds(r0 + GLA_CHUNK - 1, 1), :])
        st_ref[...] = decay * st + jnp.where(st_r == st_c, kv_t, 0.0)
        return carry

    lax.fori_loop(0, EV_CHUNKS, chunk_body, 0)

    for hh in range(GLA_HEADS):
        oh = oo_ref[:, hh * GLA_HV:(hh + 1) * GLA_HV]
        ms = jnp.mean(oh * oh, axis=-1, keepdims=True)
        gg = p_ref[:, _G + hh * GLA_HV:_G + (hh + 1) * GLA_HV]
        on = oh * lax.rsqrt(ms + RMS_EPS) * gn_ref[:, hh * GLA_HV:(hh + 1) * GLA_HV]
        mix_ref[:, RG_WIDTH + hh * GLA_HV:RG_WIDTH + (hh + 1) * GLA_HV] = (on * (gg * _sigmoid(gg))).astype(BF16)

    y = _bdot(mix_ref[...], wout_ref[...])
    z = ALPHA * x_ref[...] + _gate(mod_ref[...]) * y
    o_ref[...] = _layer_norm(z, lng_ref[...], lnb_ref[...])


def _even_layer_call(x, mod, w_in, conv_w, conv_b, wg, b_a, b_x, lam, w_up, b_up, gn, w_out, ln_g, ln_b):
    bsz, seq, _ = x.shape
    small = [w_in, conv_w, conv_b, wg, b_a, b_x, lam, w_up, b_up, gn, w_out, ln_g, ln_b]
    return pl.pallas_call(
        _even_layer_kernel,
        grid=(bsz, seq // EV_T),
        in_specs=[pl.BlockSpec((None, EV_T, D_MODEL), lambda b, t: (b, t, 0)),
                  pl.BlockSpec((None, 1, 3 * D_MODEL), lambda b, t: (b, 0, 0))]
        + [_resident(a.shape) for a in small],
        out_specs=pl.BlockSpec((None, EV_T, D_MODEL), lambda b, t: (b, t, 0)),
        out_shape=jax.ShapeDtypeStruct((bsz, seq, D_MODEL), F32),
        scratch_shapes=[
            pltpu.VMEM((EV_T, EVEN_IN_PAD), F32),
            pltpu.VMEM((EV_T, D_MODEL), BF16),
            pltpu.VMEM((SUBLANES, RG_WIDTH), F32),
            pltpu.VMEM((SUBLANES, RG_WIDTH), F32),
            pltpu.VMEM((GLA_VDIM, GLA_KDIM), F32),
            pltpu.VMEM((EV_T, GLA_KDIM), BF16),
            pltpu.VMEM((EV_T, GLA_KDIM), BF16),
            pltpu.VMEM((EV_T, GLA_KDIM), BF16),
            pltpu.VMEM((EV_T, GLA_VDIM), BF16),
            pltpu.VMEM((EV_T, GLA_KDIM), F32),
            pltpu.VMEM((EV_T, GLA_VDIM), F32),
        ],
        compiler_params=_cparams(("arbitrary", "arbitrary")),
        name="even_layer",
    )(x, mod, *small)


OUT_TM = 512


def _outproj_ln_kernel(a_ref, w_ref, x_ref, mod_ref, g_ref, b_ref, o_ref):
    y = _bdot(a_ref[...], w_ref[...])
    z = ALPHA * x_ref[...] + _gate(mod_ref[...]) * y
    o_ref[...] = _layer_norm(z, g_ref[...], b_ref[...])


def _outproj_ln_call(a, w, x, mod, ln_g, ln_b):
    bsz, seq, _ = x.shape
    return pl.pallas_call(
        _outproj_ln_kernel,
        grid=(bsz, seq // OUT_TM),
        in_specs=[
            pl.BlockSpec((None, OUT_TM, D_MODEL), lambda b, t: (b, t, 0)),
            _resident(w.shape),
            pl.BlockSpec((None, OUT_TM, D_MODEL), lambda b, t: (b, t, 0)),
            pl.BlockSpec((None, 1, 3 * D_MODEL), lambda b, t: (b, 0, 0)),
            _resident(ln_g.shape),
            _resident(ln_b.shape),
        ],
        out_specs=pl.BlockSpec((None, OUT_TM, D_MODEL), lambda b, t: (b, t, 0)),
        out_shape=jax.ShapeDtypeStruct((bsz, seq, D_MODEL), F32),
        compiler_params=_cparams(("arbitrary", "arbitrary")),
        name="outproj_ln",
    )(a, w, x, mod, ln_g, ln_b)


MLP_TM = 512
MLP_TF = 1024


def _mlp_kernel(x_ref, mod_ref, w1_ref, b1_ref, w2_ref, b2_ref, g_ref, b_ref, o_ref):
    x = x_ref[...]
    mod = mod_ref[...]
    u = _modulate(x, mod).astype(BF16)
    y = jnp.zeros((MLP_TM, D_MODEL), F32) + b2_ref[...]
    for f0 in range(0, D_FF, MLP_TF):
        hdn = _bdot(u, w1_ref[:, f0:f0 + MLP_TF]) + b1_ref[:, f0:f0 + MLP_TF]
        hdn = jnp.maximum(hdn, 0.0)
        y = y + _bdot((hdn * hdn).astype(BF16), w2_ref[f0:f0 + MLP_TF, :])
    z = ALPHA * x + _gate(mod) * y
    o_ref[...] = _layer_norm(z, g_ref[...], b_ref[...])


def _mlp_call(x, mod, w1, b1, w2, b2, ln_g, ln_b):
    bsz, seq, _ = x.shape
    return pl.pallas_call(
        _mlp_kernel,
        grid=(bsz, seq // MLP_TM),
        in_specs=[
            pl.BlockSpec((None, MLP_TM, D_MODEL), lambda b, t: (b, t, 0)),
            pl.BlockSpec((None, 1, 3 * D_MODEL), lambda b, t: (b, 0, 0)),
            _resident(w1.shape), _resident(b1.shape), _resident(w2.shape), _resident(b2.shape),
            _resident(ln_g.shape), _resident(ln_b.shape),
        ],
        out_specs=pl.BlockSpec((None, MLP_TM, D_MODEL), lambda b, t: (b, t, 0)),
        out_shape=jax.ShapeDtypeStruct((bsz, seq, D_MODEL), F32),
        compiler_params=_cparams(("arbitrary", "arbitrary")),
        name="mlp",
    )(x, mod, w1, b1, w2, b2, ln_g, ln_b)


FP_T = 512
_OQ, _OK, _OV, _OG, _OF = 0, D_MODEL, 2 * D_MODEL, 3 * D_MODEL, 4 * D_MODEL
F_ONE_COL = F_PARTS * FOX_HEADS


def _fox_place_matrices():
    pq = np.zeros((LANES, FOX_PAIRS * LANES), np.float32)
    pk = np.zeros((LANES, FOX_HEADS * LANES), np.float32)
    for h in range(FOX_HEADS):
        pair, e = divmod(h, 2)
        for part in range(F_PARTS):
            pq[part * FOX_HEADS + h, pair * LANES + F_SLOTS * e + part] = 1.0
            pq[F_ONE_COL, pair * LANES + F_SLOTS * e + F_PARTS + part] = 1.0
            pk[F_ONE_COL, h * LANES + F_SLOTS * e + part] = 1.0
            pk[part * FOX_HEADS + h, h * LANES + F_SLOTS * e + F_PARTS + part] = -1.0
    return pq, pk


def _fox_prep_kernel(x_ref, mod_ref, win_ref, bf_ref, gq_ref, gk_ref, pq_ref, pk_ref,
                     qa_ref, ka_ref, vt_ref, og_ref, p_ref, fc_ref):
    t_idx = pl.program_id(1)

    @pl.when(t_idx == 0)
    def _():
        fc_ref[...] = jnp.zeros_like(fc_ref)

    u_in = _modulate(x_ref[...], mod_ref[...]).astype(BF16)
    for c0 in range(0, ODD_IN_PAD, PROJ_TN):
        c1 = min(c0 + PROJ_TN, ODD_IN_PAD)
        p_ref[:, c0:c1] = _bdot(u_in, win_ref[:, c0:c1])
    og_ref[...] = _sigmoid(p_ref[:, _OG:_OG + D_MODEL]).astype(BF16)

    lane = lax.broadcasted_iota(jnp.int32, (FP_T, LANES), 1)
    f = _log_sigmoid(p_ref[:, _OF:_OF + LANES] + bf_ref[...])
    row = lax.broadcasted_iota(jnp.int32, (FP_T, LANES), 0)
    d = 1
    while d < FP_T:
        f = jnp.where(row >= d, f + pltpu.roll(f, d, 0), f)
        d *= 2
    f = f + fc_ref[SUBLANES - 1:SUBLANES, :]
    fc_ref[...] = f[FP_T - SUBLANES:FP_T]
    f = jnp.where(lane < FOX_HEADS, f * LOG2E, 0.0)
    f1 = f.astype(BF16).astype(F32)
    r1 = f - f1
    f2 = r1.astype(BF16).astype(F32)
    f3 = (r1 - f2).astype(BF16).astype(F32)
    packed = f1 + pltpu.roll(f2, FOX_HEADS, 1) + pltpu.roll(f3, 2 * FOX_HEADS, 1)
    packed = jnp.where(lane == F_ONE_COL, 1.0, packed).astype(BF16)
    fq = _bdot(packed, pq_ref[...])
    fk = _bdot(packed, pk_ref[...])

    left = lane < FOX_HD

    def head_norm(xp, g):
        sq = xp * xp
        ms_l = jnp.sum(jnp.where(left, sq, 0.0), axis=-1, keepdims=True)
        ms_r = jnp.sum(jnp.where(left, 0.0, sq), axis=-1, keepdims=True)
        ms = jnp.where(left, ms_l, ms_r) * (1.0 / FOX_HD)
        return xp * lax.rsqrt(ms + RMS_EPS) * g

    for pair in range(FOX_PAIRS):
        c0 = pair * LANES
        qn = head_norm(p_ref[:, _OQ + c0:_OQ + c0 + LANES], gq_ref[...]) * (FOX_HD ** -0.5 * LOG2E)
        kn = head_norm(p_ref[:, _OK + c0:_OK + c0 + LANES], gk_ref[...])
        vv = p_ref[:, _OV + c0:_OV + c0 + LANES]
        qa_ref[pair, :, 0:LANES] = qn.astype(BF16)
        qa_ref[pair, :, LANES:2 * LANES] = fq[:, c0:c0 + LANES].astype(BF16)
        for e in range(2):
            h = 2 * pair + e
            own = left if e == 0 else jnp.logical_not(left)
            ka_ref[h, :, 0:LANES] = jnp.where(own, kn, 0.0).astype(BF16)
            ka_ref[h, :, LANES:2 * LANES] = fk[:, h * LANES:(h + 1) * LANES].astype(BF16)
            vt_ref[h, 0] = jnp.where(own, vv, 1.0).T.astype(BF16)


def _fox_prep_call(x, mod, w_in, b_f, gq, gk, pq, pk):
    bsz, seq, _ = x.shape
    small = [w_in, b_f, gq, gk, pq, pk]
    return pl.pallas_call(
        _fox_prep_kernel,
        grid=(bsz, seq // FP_T),
        in_specs=[pl.BlockSpec((None, FP_T, D_MODEL), lambda b, t: (b, t, 0)),
                  pl.BlockSpec((None, 1, 3 * D_MODEL), lambda b, t: (b, 0, 0))]
        + [_resident(a.shape) for a in small],
        out_specs=[
            pl.BlockSpec((None, FOX_PAIRS, FP_T, 2 * LANES), lambda b, t: (b, 0, t, 0)),
            pl.BlockSpec((None, FOX_HEADS, FP_T, 2 * LANES), lambda b, t: (b, 0, t, 0)),
            pl.BlockSpec((None, FOX_HEADS, 1, LANES, FP_T), lambda b, t: (b, 0, t, 0, 0)),
            pl.BlockSpec((None, FP_T, D_MODEL), lambda b, t: (b, t, 0)),
        ],
        out_shape=[
            jax.ShapeDtypeStruct((bsz, FOX_PAIRS, seq, 2 * LANES), BF16),
            jax.ShapeDtypeStruct((bsz, FOX_HEADS, seq, 2 * LANES), BF16),
            jax.ShapeDtypeStruct((bsz, FOX_HEADS, seq // FP_T, LANES, FP_T), BF16),
            jax.ShapeDtypeStruct((bsz, seq, D_MODEL), BF16),
        ],
        scratch_shapes=[
            pltpu.VMEM((FP_T, ODD_IN_PAD), F32),
            pltpu.VMEM((SUBLANES, LANES), F32),
        ],
        compiler_params=_cparams(("arbitrary", "arbitrary")),
        name="fox_prep",
    )(x, mod, *small)


FA_T = FP_T
FA_NH = 4
FA_NP = FA_NH // 2


def _fox_attn_kernel(qa_ref, ka_ref, vt_ref, g_ref, o_ref, m_ref, acc_ref, sa_ref, sb_ref):
    qi = pl.program_id(2)
    nt = (((1,), (1,)), ((), ()))
    key = lax.broadcasted_iota(jnp.int32, (FA_T, FA_T), 0)
    qry = lax.broadcasted_iota(jnp.int32, (FA_T, FA_T), 1)
    m_ref[...] = jnp.full_like(m_ref, MASK_NEG)
    acc_ref[...] = jnp.zeros_like(acc_ref)

    def scores(j, dst_ref, h):
        k0 = pl.multiple_of(j * FA_T, FA_T)
        dst_ref[h] = lax.dot_general(ka_ref[h, pl.ds(k0, FA_T), :], qa_ref[h // 2], nt,
                                     preferred_element_type=F32)

    def softmax_pv(j, src_ref, h, masked):
        s_t = src_ref[h]
        if masked:
            s_t = jnp.where(key <= qry, s_t, MASK_NEG)
        m_prev = m_ref[h]
        m_new = jnp.maximum(m_prev, jnp.max(s_t, axis=0, keepdims=True))
        p_t = jnp.exp2(s_t - m_new).astype(BF16)
        acc_ref[h] = jnp.exp2(m_prev - m_new) * acc_ref[h] + _bdot(vt_ref[h, j], p_t)
        m_ref[h] = m_new

    def pipelined_step(j, cur_ref, nxt_ref):
        scores(j + 1, nxt_ref, 0)
        for h in range(FA_NH):
            if h + 1 < FA_NH:
                scores(j + 1, nxt_ref, h + 1)
            softmax_pv(j, cur_ref, h, False)

    def masked_step(j, cur_ref):
        for h in range(FA_NH):
            softmax_pv(j, cur_ref, h, True)

    for h in range(FA_NH):
        scores(0, sa_ref, h)

    def pair_body(i, carry):
        pipelined_step(2 * i, sa_ref, sb_ref)
        pipelined_step(2 * i + 1, sb_ref, sa_ref)
        return carry

    lax.fori_loop(0, qi // 2, pair_body, 0)

    @pl.when(qi % 2 == 1)
    def _():
        pipelined_step(qi - 1, sa_ref, sb_ref)
        masked_step(qi, sb_ref)

    @pl.when(qi % 2 == 0)
    def _():
        masked_step(qi, sa_ref)

    top = lax.broadcasted_iota(jnp.int32, (LANES, FA_T), 0) < FOX_HD
    for p in range(FA_NP):
        acc0 = acc_ref[2 * p]
        acc1 = acc_ref[2 * p + 1]
        o_t = jnp.where(top, acc0 / acc0[FOX_HD:FOX_HD + 1, :], acc1 / acc1[0:1, :])
        g = g_ref[:, p * LANES:(p + 1) * LANES].astype(F32)
        o_ref[:, p * LANES:(p + 1) * LANES] = (o_t.T * g).astype(BF16)


def _fox_attn_call(qa, ka, vt, og):
    bsz, _, seq, _ = qa.shape
    gw = FA_NP * LANES
    return pl.pallas_call(
        _fox_attn_kernel,
        grid=(bsz, FOX_HEADS // FA_NH, seq // FA_T),
        in_specs=[
            pl.BlockSpec((None, FA_NP, FA_T, 2 * LANES), lambda b, p, t: (b, p, t, 0)),
            pl.BlockSpec((None, FA_NH, seq, 2 * LANES), lambda b, p, t: (b, p, 0, 0)),
            pl.BlockSpec((None, FA_NH, seq // FA_T, LANES, FA_T), lambda b, p, t: (b, p, 0, 0, 0)),
            pl.BlockSpec((None, FA_T, gw), lambda b, p, t: (b, t, p)),
        ],
        out_specs=pl.BlockSpec((None, FA_T, gw), lambda b, p, t: (b, t, p)),
        out_shape=jax.ShapeDtypeStruct((bsz, seq, D_MODEL), BF16),
        scratch_shapes=[
            pltpu.VMEM((FA_NH, 1, FA_T), F32),
            pltpu.VMEM((FA_NH, LANES, FA_T), F32),
            pltpu.VMEM((FA_NH, FA_T, FA_T), F32),
            pltpu.VMEM((FA_NH, FA_T, FA_T), F32),
        ],
        compiler_params=_cparams(("arbitrary", "arbitrary", "arbitrary")),
        name="fox_attn",
    )(qa, ka, vt, og)


def _pad_cols(w, n):
    return jnp.pad(w, ((0, 0), (0, n - w.shape[1])))


def _gate_weights(w_a, w_x):
    per_half = RG_BLOCKS // 2
    half = RG_WIDTH // 2

    def bd(w, j):
        m = jnp.zeros((half, half), F32)
        for i in range(per_half):
            m = lax.dynamic_update_slice(m, w[j * per_half + i], (i * RG_BLOCK, i * RG_BLOCK))
        return m

    return jnp.stack([jnp.concatenate([bd(w_a, j), bd(w_x, j)], axis=1) for j in range(2)]).astype(BF16)


def kernel(x, c, ada_w, ada_b, ln_g, ln_b, ev_w_in, ev_conv_w, ev_conv_b, ev_rg_wa, ev_rg_ba, ev_rg_wx,
           ev_rg_bx, ev_rg_lam, ev_gla_w_up, ev_gla_b_up, ev_gla_norm_g, ev_w_out, od_w_in, od_b_f,
           od_q_norm_g, od_k_norm_g, od_w_out, mlp_w1, mlp_b1, mlp_w2, mlp_b2):
    bsz = x.shape[0]
    mods = _ada_call(c, ada_w.reshape(2 * DEPTH, D_MODEL, 3 * D_MODEL),
                     ada_b.reshape(2 * DEPTH, 1, 3 * D_MODEL))
    mods = mods.reshape(2 * DEPTH, bsz, 1, 3 * D_MODEL)
    pq, pk = _fox_place_matrices()
    pq = jnp.asarray(pq, BF16)
    pk = jnp.asarray(pk, BF16)

    for layer in range(DEPTH):
        mod_mix = mods[2 * layer]
        mod_mlp = mods[2 * layer + 1]
        if layer % 2 == 0:
            e = layer // 2
            w_in = _pad_cols(ev_w_in[e], EVEN_IN_PAD).astype(BF16)
            w_up = jnp.pad(ev_gla_w_up[e], ((0, LANES - GLA_LOWRANK), (0, 0)))
            x = _even_layer_call(
                x, mod_mix, w_in, ev_conv_w[e], ev_conv_b[e][None], _gate_weights(ev_rg_wa[e], ev_rg_wx[e]),
                ev_rg_ba[e][None], ev_rg_bx[e][None], ev_rg_lam[e][None], w_up, ev_gla_b_up[e][None],
                ev_gla_norm_g[e][None], ev_w_out[e].astype(BF16), ln_g[layer, 0][None], ln_b[layer, 0][None])
        else:
            o = layer // 2
            w_in = _pad_cols(od_w_in[o], ODD_IN_PAD).astype(BF16)
            b_f = jnp.pad(od_b_f[o], (0, LANES - FOX_HEADS))[None]
            gq = jnp.tile(od_q_norm_g[o], 2)[None]
            gk = jnp.tile(od_k_norm_g[o], 2)[None]
            qa, ka, vt, og = _fox_prep_call(x, mod_mix, w_in, b_f, gq, gk, pq, pk)
            mix = _fox_attn_call(qa, ka, vt, og)
            x = _outproj_ln_call(mix, od_w_out[o].astype(BF16), x, mod_mix,
                                 ln_g[layer, 0][None], ln_b[layer, 0][None])
        x = _mlp_call(x, mod_mlp, mlp_w1[layer].astype(BF16), mlp_b1[layer][None],
                      mlp_w2[layer].astype(BF16), mlp_b2[layer][None],
                      ln_g[layer, 1][None], ln_b[layer, 1][None])
    return x
```

```python
import numpy as np
import jax
import jax.numpy as jnp
from jax import lax
from jax.experimental import pallas as pl
from jax.experimental.pallas import tpu as pltpu

F32 = jnp.float32
BF16 = jnp.bfloat16

D_MODEL = 1024
DEPTH = 2
RG_WIDTH = D_MODEL // 2
RG_BLOCKS = 8
RG_BLOCK = RG_WIDTH // RG_BLOCKS
CONV_WIDTH = 4
RG_C = 8.0
GLA_HEADS = 4
GLA_VDIM = D_MODEL // 2
GLA_KDIM = GLA_VDIM // 2
GLA_HK = GLA_KDIM // GLA_HEADS
GLA_HV = GLA_VDIM // GLA_HEADS
GLA_LOWRANK = 16
GLA_GATE_TAU = 16.0
GLA_CHUNK = 64
EVEN_IN = 2 * RG_WIDTH + 2 * GLA_KDIM + 2 * GLA_VDIM + GLA_LOWRANK
FOX_HEADS = 16
FOX_HD = D_MODEL // FOX_HEADS
FOX_PAIRS = FOX_HEADS // 2
ODD_IN = 4 * D_MODEL + FOX_HEADS
D_FF = 4 * D_MODEL
ALPHA = (2 * DEPTH) ** 0.25
LN_EPS = 1e-5
RMS_EPS = 1e-6

LANES = 128
SUBLANES = 8
VMEM_LIMIT = 56 * 1024 * 1024

EVEN_IN_PAD = EVEN_IN + (LANES - GLA_LOWRANK)
ODD_IN_PAD = ODD_IN + (LANES - FOX_HEADS)
MASK_NEG = -1e30
LOG2E = float(np.log2(np.e))

F_PARTS = 3
F_SLOTS = 8

PROJ_TN = 512


def _cparams(sem):
    return pltpu.CompilerParams(dimension_semantics=sem, vmem_limit_bytes=VMEM_LIMIT)


def _resident(shape):
    nd = len(shape)
    return pl.BlockSpec(shape, lambda *_: (0,) * nd, pipeline_mode=pl.Buffered(1))


def _sigmoid(x):
    return 0.5 * jnp.tanh(0.5 * x) + 0.5


def _log_sigmoid(x):
    return jnp.minimum(x, 0.0) - jnp.log1p(jnp.exp(-jnp.abs(x)))


def _gelu_tanh(x):
    c = np.sqrt(2.0 / np.pi).astype(np.float32)
    return x * (0.5 * (1.0 + jnp.tanh(c * (x + 0.044715 * (x * x * x)))))


def _sqrt_nonneg(y):
    return jnp.where(y > 0.0, y * lax.rsqrt(y), 0.0)


def _bdot(a, b):
    return jnp.dot(a, b, preferred_element_type=F32)


def _split2(a):
    hi = a.astype(BF16)
    lo = (a - hi.astype(F32)).astype(BF16)
    return hi, lo


def _dot_split(a, w):
    ah, al = _split2(a)
    wh, wl = _split2(w)
    return _bdot(ah, wh) + (_bdot(ah, wl) + _bdot(al, wh))


def _layer_norm(z, g, b):
    mu = jnp.mean(z, axis=-1, keepdims=True)
    zc = z - mu
    var = jnp.mean(zc * zc, axis=-1, keepdims=True)
    return zc * lax.rsqrt(var + LN_EPS) * g + b


def _modulate(x, mod):
    shift = mod[:, 0:D_MODEL]
    scale = mod[:, D_MODEL:2 * D_MODEL]
    return x * (1.0 + scale) + shift


def _gate(mod):
    return 1.0 + mod[:, 2 * D_MODEL:3 * D_MODEL]


MLP_TF = 1024


def _mlp_ln(x, mod, w1_ref, b1_ref, w2_ref, b2_ref, g_ref, b_ref):
    u = _modulate(x, mod).astype(BF16)
    y = jnp.zeros(x.shape, F32) + b2_ref[...]
    for f0 in range(0, D_FF, MLP_TF):
        hdn = _bdot(u, w1_ref[:, f0:f0 + MLP_TF]) + b1_ref[:, f0:f0 + MLP_TF]
        hdn = jnp.maximum(hdn, 0.0)
        y = y + _bdot((hdn * hdn).astype(BF16), w2_ref[f0:f0 + MLP_TF, :])
    z = ALPHA * x + _gate(mod) * y
    return _layer_norm(z, g_ref[...], b_ref[...])


ADA_TN = 1024


def _ada_kernel(c_ref, w_ref, b_ref, o_ref):
    c = c_ref[...]
    s = c * _sigmoid(c)
    o_ref[...] = _dot_split(s, w_ref[...]) + b_ref[...]


def _ada_call(c, ada_w, ada_b):
    n_mod = ada_w.shape[0]
    bsz = c.shape[0]
    return pl.pallas_call(
        _ada_kernel,
        grid=(n_mod, 3 * D_MODEL // ADA_TN),
        in_specs=[
            pl.BlockSpec((bsz, D_MODEL), lambda i, j: (0, 0)),
            pl.BlockSpec((None, D_MODEL, ADA_TN), lambda i, j: (i, 0, j)),
            pl.BlockSpec((None, 1, ADA_TN), lambda i, j: (i, 0, j)),
        ],
        out_specs=pl.BlockSpec((None, bsz, ADA_TN), lambda i, j: (i, 0, j)),
        out_shape=jax.ShapeDtypeStruct((n_mod, bsz, 3 * D_MODEL), F32),
        compiler_params=_cparams(("arbitrary", "arbitrary")),
        name="ada",
    )(c, ada_w, ada_b)


EV_T = 512
EV_CHUNKS = EV_T // GLA_CHUNK
_XR, _YR = 0, RG_WIDTH
_Q = 2 * RG_WIDTH
_K = _Q + GLA_KDIM
_V = _K + GLA_KDIM
_G = _V + GLA_VDIM
_ZL = _G + GLA_VDIM


def _scan_rows8(a, u):
    rows, ch = a.shape
    a3 = a.reshape(rows // SUBLANES, SUBLANES, ch)
    u3 = u.reshape(rows // SUBLANES, SUBLANES, ch)
    sub = lax.broadcasted_iota(jnp.int32, a3.shape, 1)
    d = 1
    while d < SUBLANES:
        keep = sub >= d
        u3 = jnp.where(keep, a3 * pltpu.roll(u3, d, 1) + u3, u3)
        a3 = jnp.where(keep, a3 * pltpu.roll(a3, d, 1), a3)
        d *= 2
    return a3, u3


def _even_layer_kernel(x_ref, mod_ref, win_ref, convw_ref, convb_ref, wg_ref, ba_ref, bx_ref, lam_ref,
                       wup_ref, bup_ref, gn_ref, wout_ref, lng_ref, lnb_ref,
                       mod2_ref, w1_ref, b1_ref, w2_ref, b2_ref, lng2_ref, lnb2_ref, o_ref,
                       p_ref, mix_ref, tail_ref, hc_ref, st_ref, qd_ref, kd_ref, ke_ref, v_ref, bc_ref,
                       oo_ref):
    t_idx = pl.program_id(1)

    @pl.when(t_idx == 0)
    def _():
        tail_ref[...] = jnp.zeros_like(tail_ref)
        hc_ref[...] = jnp.zeros_like(hc_ref)
        st_ref[...] = jnp.zeros_like(st_ref)

    u_in = _modulate(x_ref[...], mod_ref[...]).astype(BF16)
    for c0 in range(0, EVEN_IN_PAD, PROJ_TN):
        c1 = min(c0 + PROJ_TN, EVEN_IN_PAD)
        p_ref[:, c0:c1] = _bdot(u_in, win_ref[:, c0:c1])

    xr = p_ref[:, _XR:_XR + RG_WIDTH]
    tail = tail_ref[...]
    row8 = lax.broadcasted_iota(jnp.int32, (SUBLANES, RG_WIDTH), 0)
    xc = convb_ref[...] + convw_ref[CONV_WIDTH - 1:CONV_WIDTH, :] * xr
    for j in range(1, CONV_WIDTH):
        xs = pltpu.roll(xr, j, 0)
        head = jnp.where(row8 < j, pltpu.roll(tail, j, 0), xs[0:SUBLANES])
        xs = jnp.concatenate([head, xs[SUBLANES:]], axis=0)
        xc = xc + convw_ref[CONV_WIDTH - 1 - j:CONV_WIDTH - j, :] * xs
    tail_ref[...] = xr[EV_T - SUBLANES:EV_T]

    half = RG_WIDTH // 2
    r_parts, i_parts = [], []
    for j in range(2):
        ri = _bdot(xc[:, j * half:(j + 1) * half].astype(BF16), wg_ref[j])
        r_parts.append(ri[:, :half])
        i_parts.append(ri[:, half:])
    r = _sigmoid(jnp.concatenate(r_parts, axis=1) + ba_ref[...])
    ig = _sigmoid(jnp.concatenate(i_parts, axis=1) + bx_ref[...])
    log_a = RG_C * r * _log_sigmoid(lam_ref[...])
    a = jnp.exp(log_a)
    u = _sqrt_nonneg(-jnp.tanh(log_a) * (a * a + 1.0)) * (ig * xc)

    a3, h3 = _scan_rows8(a, u)
    carry = jnp.broadcast_to(hc_ref[SUBLANES - 1:SUBLANES, :], (SUBLANES, RG_WIDTH))
    h_groups = []
    for g in range(EV_T // SUBLANES):
        hg = h3[g] + a3[g] * carry
        h_groups.append(hg)
        carry = jnp.broadcast_to(hg[SUBLANES - 1:SUBLANES, :], (SUBLANES, RG_WIDTH))
    hc_ref[...] = carry
    h = jnp.concatenate(h_groups, axis=0)
    mix_ref[:, 0:RG_WIDTH] = (h * _gelu_tanh(p_ref[:, _YR:_YR + RG_WIDTH])).astype(BF16)

    z = _dot_split(p_ref[:, _ZL:_ZL + LANES], wup_ref[...]) + bup_ref[...]
    bc = _log_sigmoid(z) * (1.0 / GLA_GATE_TAU)
    rowk = lax.broadcasted_iota(jnp.int32, (EV_T, GLA_KDIM), 0) & (GLA_CHUNK - 1)
    d = 1
    while d < GLA_CHUNK:
        bc = jnp.where(rowk >= d, bc + pltpu.roll(bc, d, 0), bc)
        d *= 2
    bc3 = bc.reshape(EV_CHUNKS, GLA_CHUNK, GLA_KDIM)
    b_last = jnp.broadcast_to(bc3[:, GLA_CHUNK - 1:GLA_CHUNK, :], bc3.shape).reshape(EV_T, GLA_KDIM)
    q = p_ref[:, _Q:_Q + GLA_KDIM]
    k = p_ref[:, _K:_K + GLA_KDIM]
    qd_ref[...] = (q * (GLA_HK ** -0.5) * jnp.exp(bc)).astype(BF16)
    kd_ref[...] = (k * jnp.exp(-bc)).astype(BF16)
    ke_ref[...] = (k * jnp.exp(b_last - bc)).astype(BF16)
    v_ref[...] = p_ref[:, _V:_V + GLA_VDIM].astype(BF16)
    bc_ref[...] = bc

    lane_k = lax.broadcasted_iota(jnp.int32, (GLA_CHUNK, GLA_KDIM), 1) // GLA_HK
    lane_v = lax.broadcasted_iota(jnp.int32, (GLA_CHUNK, GLA_VDIM), 1) // GLA_HV
    tri_r = lax.broadcasted_iota(jnp.int32, (GLA_CHUNK, GLA_HEADS * GLA_CHUNK), 0)
    tri_c = lax.broadcasted_iota(jnp.int32, (GLA_CHUNK, GLA_HEADS * GLA_CHUNK), 1) & (GLA_CHUNK - 1)
    st_r = lax.broadcasted_iota(jnp.int32, (GLA_VDIM, GLA_KDIM), 0) // GLA_HV
    st_c = lax.broadcasted_iota(jnp.int32, (GLA_VDIM, GLA_KDIM), 1) // GLA_HK
    nt = (((1,), (1,)), ((), ()))
    tn = (((0,), (0,)), ((), ()))

    def chunk_body(c, carry):
        r0 = pl.multiple_of(c * GLA_CHUNK, GLA_CHUNK)
        qd = qd_ref[pl.ds(r0, GLA_CHUNK), :]
        kd = kd_ref[pl.ds(r0, GLA_CHUNK), :]
        ke = ke_ref[pl.ds(r0, GLA_CHUNK), :]
        vv = v_ref[pl.ds(r0, GLA_CHUNK), :]
        zk = jnp.zeros_like(kd)
        zv = jnp.zeros_like(vv)
        kbd = jnp.concatenate([jnp.where(lane_k == hh, kd, zk) for hh in range(GLA_HEADS)], axis=0)
        att = lax.dot_general(qd, kbd, nt, preferred_element_type=F32)
        att = jnp.where(tri_c <= tri_r, att, 0.0).astype(BF16)
        vbd = jnp.concatenate([jnp.where(lane_v == hh, vv, zv) for hh in range(GLA_HEADS)], axis=0)
        st = st_ref[...]
        o = _bdot(att, vbd) + lax.dot_general(qd, st.astype(BF16), nt, preferred_element_type=F32)
        oo_ref[pl.ds(r0, GLA_CHUNK), :] = o
        kv_t = lax.dot_general(vv, ke, tn, preferred_element_type=F32)
        decay = jnp.exp(bc_ref[pl.ds(r0 + GLA_CHUNK - 1, 1), :])
        st_ref[...] = decay * st + jnp.where(st_r == st_c, kv_t, 0.0)
        return carry

    lax.fori_loop(0, EV_CHUNKS, chunk_body, 0)

    for hh in range(GLA_HEADS):
        oh = oo_ref[:, hh * GLA_HV:(hh + 1) * GLA_HV]
        ms = jnp.mean(oh * oh, axis=-1, keepdims=True)
        gg = p_ref[:, _G + hh * GLA_HV:_G + (hh + 1) * GLA_HV]
        on = oh * lax.rsqrt(ms + RMS_EPS) * gn_ref[:, hh * GLA_HV:(hh + 1) * GLA_HV]
        mix_ref[:, RG_WIDTH + hh * GLA_HV:RG_WIDTH + (hh + 1) * GLA_HV] = (on * (gg * _sigmoid(gg))).astype(BF16)

    y = _bdot(mix_ref[...], wout_ref[...])
    z = ALPHA * x_ref[...] + _gate(mod_ref[...]) * y
    x_mid = _layer_norm(z, lng_ref[...], lnb_ref[...])
    o_ref[...] = _mlp_ln(x_mid, mod2_ref[...], w1_ref, b1_ref, w2_ref, b2_ref, lng2_ref, lnb2_ref)


def _even_layer_call(x, mod, mixer_params, mod_mlp, mlp_params):
    bsz, seq, _ = x.shape
    mod_spec = pl.BlockSpec((None, 1, 3 * D_MODEL), lambda b, t: (b, 0, 0))
    return pl.pallas_call(
        _even_layer_kernel,
        grid=(bsz, seq // EV_T),
        in_specs=[pl.BlockSpec((None, EV_T, D_MODEL), lambda b, t: (b, t, 0)), mod_spec]
        + [_resident(a.shape) for a in mixer_params] + [mod_spec]
        + [_resident(a.shape) for a in mlp_params],
        out_specs=pl.BlockSpec((None, EV_T, D_MODEL), lambda b, t: (b, t, 0)),
        out_shape=jax.ShapeDtypeStruct((bsz, seq, D_MODEL), F32),
        scratch_shapes=[
            pltpu.VMEM((EV_T, EVEN_IN_PAD), F32),
            pltpu.VMEM((EV_T, D_MODEL), BF16),
            pltpu.VMEM((SUBLANES, RG_WIDTH), F32),
            pltpu.VMEM((SUBLANES, RG_WIDTH), F32),
            pltpu.VMEM((GLA_VDIM, GLA_KDIM), F32),
            pltpu.VMEM((EV_T, GLA_KDIM), BF16),
            pltpu.VMEM((EV_T, GLA_KDIM), BF16),
            pltpu.VMEM((EV_T, GLA_KDIM), BF16),
            pltpu.VMEM((EV_T, GLA_VDIM), BF16),
            pltpu.VMEM((EV_T, GLA_KDIM), F32),
            pltpu.VMEM((EV_T, GLA_VDIM), F32),
        ],
        compiler_params=_cparams(("arbitrary", "arbitrary")),
        name="even_layer",
    )(x, mod, *mixer_params, mod_mlp, *mlp_params)


POST_TM = 512


def _post_kernel(a_ref, wout_ref, x_ref, mod_ref, lng_ref, lnb_ref,
                 mod2_ref, w1_ref, b1_ref, w2_ref, b2_ref, lng2_ref, lnb2_ref, o_ref):
    y = _bdot(a_ref[...], wout_ref[...])
    z = ALPHA * x_ref[...] + _gate(mod_ref[...]) * y
    x_mid = _layer_norm(z, lng_ref[...], lnb_ref[...])
    o_ref[...] = _mlp_ln(x_mid, mod2_ref[...], w1_ref, b1_ref, w2_ref, b2_ref, lng2_ref, lnb2_ref)


def _post_call(a, w_out, x, mod, ln_g, ln_b, mod_mlp, mlp_params):
    bsz, seq, _ = x.shape
    tile = pl.BlockSpec((None, POST_TM, D_MODEL), lambda b, t: (b, t, 0))
    mod_spec = pl.BlockSpec((None, 1, 3 * D_MODEL), lambda b, t: (b, 0, 0))
    return pl.pallas_call(
        _post_kernel,
        grid=(bsz, seq // POST_TM),
        in_specs=[tile, _resident(w_out.shape), tile, mod_spec, _resident(ln_g.shape), _resident(ln_b.shape),
                  mod_spec] + [_resident(p.shape) for p in mlp_params],
        out_specs=tile,
        out_shape=jax.ShapeDtypeStruct((bsz, seq, D_MODEL), F32),
        compiler_params=_cparams(("arbitrary", "arbitrary")),
        name="post",
    )(a, w_out, x, mod, ln_g, ln_b, mod_mlp, *mlp_params)


FP_T = 512
_OQ, _OK, _OV, _OG, _OF = 0, D_MODEL, 2 * D_MODEL, 3 * D_MODEL, 4 * D_MODEL
F_ONE_COL = F_PARTS * FOX_HEADS


def _fox_place_matrices():
    pq = np.zeros((LANES, FOX_HEADS * LANES), np.float32)
    pk = np.zeros((LANES, FOX_HEADS * LANES), np.float32)
    for h in range(FOX_HEADS):
        base = h * LANES + (FOX_HD if h % 2 == 0 else 0)
        for part in range(F_PARTS):
            pq[part * FOX_HEADS + h, base + part] = 1.0
            pq[F_ONE_COL, base + F_PARTS + part] = 1.0
            pk[F_ONE_COL, base + part] = 1.0
            pk[part * FOX_HEADS + h, base + F_PARTS + part] = -1.0
    return pq, pk


def _fox_prep_kernel(x_ref, mod_ref, win_ref, bf_ref, gq_ref, gk_ref, pq_ref, pk_ref,
                     qa_ref, ka_ref, vt_ref, og_ref, p_ref, fc_ref):
    t_idx = pl.program_id(1)

    @pl.when(t_idx == 0)
    def _():
        fc_ref[...] = jnp.zeros_like(fc_ref)

    u_in = _modulate(x_ref[...], mod_ref[...]).astype(BF16)
    for c0 in range(0, ODD_IN_PAD, PROJ_TN):
        c1 = min(c0 + PROJ_TN, ODD_IN_PAD)
        p_ref[:, c0:c1] = _bdot(u_in, win_ref[:, c0:c1])
    og_ref[...] = _sigmoid(p_ref[:, _OG:_OG + D_MODEL]).astype(BF16)

    lane = lax.broadcasted_iota(jnp.int32, (FP_T, LANES), 1)
    f = _log_sigmoid(p_ref[:, _OF:_OF + LANES] + bf_ref[...])
    row = lax.broadcasted_iota(jnp.int32, (FP_T, LANES), 0)
    d = 1
    while d < FP_T:
        f = jnp.where(row >= d, f + pltpu.roll(f, d, 0), f)
        d *= 2
    f = f + fc_ref[SUBLANES - 1:SUBLANES, :]
    fc_ref[...] = f[FP_T - SUBLANES:FP_T]
    f = jnp.where(lane < FOX_HEADS, f * LOG2E, 0.0)
    f1 = f.astype(BF16).astype(F32)
    r1 = f - f1
    f2 = r1.astype(BF16).astype(F32)
    f3 = (r1 - f2).astype(BF16).astype(F32)
    packed = f1 + pltpu.roll(f2, FOX_HEADS, 1) + pltpu.roll(f3, 2 * FOX_HEADS, 1)
    packed = jnp.where(lane == F_ONE_COL, 1.0, packed).astype(BF16)
    fq = _bdot(packed, pq_ref[...])
    fk = _bdot(packed, pk_ref[...])

    left = lane < FOX_HD

    def head_norm(xp, g):
        sq = xp * xp
        ms_l = jnp.sum(jnp.where(left, sq, 0.0), axis=-1, keepdims=True)
        ms_r = jnp.sum(jnp.where(left, 0.0, sq), axis=-1, keepdims=True)
        ms = jnp.where(left, ms_l, ms_r) * (1.0 / FOX_HD)
        return xp * lax.rsqrt(ms + RMS_EPS) * g

    for pair in range(FOX_PAIRS):
        c0 = pair * LANES
        qn = head_norm(p_ref[:, _OQ + c0:_OQ + c0 + LANES], gq_ref[...]) * (FOX_HD ** -0.5 * LOG2E)
        kn = head_norm(p_ref[:, _OK + c0:_OK + c0 + LANES], gk_ref[...])
        vv = p_ref[:, _OV + c0:_OV + c0 + LANES]
        for e in range(2):
            h = 2 * pair + e
            own = left if e == 0 else jnp.logical_not(left)
            qa_ref[h] = jnp.where(own, qn, fq[:, h * LANES:(h + 1) * LANES]).astype(BF16)
            ka_ref[h] = jnp.where(own, kn, fk[:, h * LANES:(h + 1) * LANES]).astype(BF16)
            vt_ref[h, 0] = jnp.where(own, vv, 1.0).T.astype(BF16)


def _fox_prep_call(x, mod, w_in, b_f, gq, gk, pq, pk):
    bsz, seq, _ = x.shape
    small = [w_in, b_f, gq, gk, pq, pk]
    return pl.pallas_call(
        _fox_prep_kernel,
        grid=(bsz, seq // FP_T),
        in_specs=[pl.BlockSpec((None, FP_T, D_MODEL), lambda b, t: (b, t, 0)),
                  pl.BlockSpec((None, 1, 3 * D_MODEL), lambda b, t: (b, 0, 0))]
        + [_resident(a.shape) for a in small],
        out_specs=[
            pl.BlockSpec((None, FOX_HEADS, FP_T, LANES), lambda b, t: (b, 0, t, 0)),
            pl.BlockSpec((None, FOX_HEADS, FP_T, LANES), lambda b, t: (b, 0, t, 0)),
            pl.BlockSpec((None, FOX_HEADS, 1, LANES, FP_T), lambda b, t: (b, 0, t, 0, 0)),
            pl.BlockSpec((None, FP_T, D_MODEL), lambda b, t: (b, t, 0)),
        ],
        out_shape=[
            jax.ShapeDtypeStruct((bsz, FOX_HEADS, seq, LANES), BF16),
            jax.ShapeDtypeStruct((bsz, FOX_HEADS, seq, LANES), BF16),
            jax.ShapeDtypeStruct((bsz, FOX_HEADS, seq // FP_T, LANES, FP_T), BF16),
            jax.ShapeDtypeStruct((bsz, seq, D_MODEL), BF16),
        ],
        scratch_shapes=[
            pltpu.VMEM((FP_T, ODD_IN_PAD), F32),
            pltpu.VMEM((SUBLANES, LANES), F32),
        ],
        compiler_params=_cparams(("arbitrary", "arbitrary")),
        name="fox_prep",
    )(x, mod, *small)


FA_T = FP_T
FA_NH = 4
FA_NP = FA_NH // 2


def _fox_attn_kernel(qa_ref, ka_ref, vt_ref, g_ref, o_ref, m_ref, acc_ref, sa_ref, sb_ref):
    seq = qa_ref.shape[1]
    n_q = seq // FA_T
    n_items = n_q * (n_q + 1) // 2
    nt = (((1,), (1,)), ((), ()))
    key = lax.broadcasted_iota(jnp.int32, (FA_T, FA_T), 0)
    qry = lax.broadcasted_iota(jnp.int32, (FA_T, FA_T), 1)
    top = lax.broadcasted_iota(jnp.int32, (LANES, FA_T), 0) < FOX_HD

    def reset_stats():
        m_ref[...] = jnp.full_like(m_ref, MASK_NEG)
        acc_ref[...] = jnp.zeros_like(acc_ref)

    def scores(qi, j, dst_ref, h):
        q0 = pl.multiple_of(qi * FA_T, FA_T)
        k0 = pl.multiple_of(j * FA_T, FA_T)
        dst_ref[h] = lax.dot_general(ka_ref[h, pl.ds(k0, FA_T), :], qa_ref[h, pl.ds(q0, FA_T), :], nt,
                                     preferred_element_type=F32)

    def softmax_pv(j, src_ref, h, masked):
        s_t = src_ref[h]
        if masked:
            s_t = jnp.where(key <= qry, s_t, MASK_NEG)
        m_prev = m_ref[h]
        m_new = jnp.maximum(m_prev, jnp.max(s_t, axis=0, keepdims=True))
        p_t = jnp.exp2(s_t - m_new).astype(BF16)
        acc_ref[h] = jnp.exp2(m_prev - m_new) * acc_ref[h] + _bdot(vt_ref[h, j], p_t)
        m_ref[h] = m_new

    def finish_tile(qi):
        q0 = pl.multiple_of(qi * FA_T, FA_T)
        for p in range(FA_NP):
            acc0 = acc_ref[2 * p]
            acc1 = acc_ref[2 * p + 1]
            o_t = jnp.where(top, acc0 * (1.0 / acc0[FOX_HD:FOX_HD + 1, :]), acc1 * (1.0 / acc1[0:1, :]))
            g = g_ref[pl.ds(q0, FA_T), p * LANES:(p + 1) * LANES].astype(F32)
            o_ref[pl.ds(q0, FA_T), p * LANES:(p + 1) * LANES] = (o_t.T * g).astype(BF16)
        reset_stats()

    def item(qi, j, cur_ref, nxt_ref):
        diag = j == qi
        qi_n = jnp.where(diag, jnp.minimum(qi + 1, n_q - 1), qi)
        j_n = jnp.where(diag, 0, j + 1)

        def run(masked):
            scores(qi_n, j_n, nxt_ref, 0)
            for h in range(FA_NH):
                if h + 1 < FA_NH:
                    scores(qi_n, j_n, nxt_ref, h + 1)
                softmax_pv(j, cur_ref, h, masked)
            if masked:
                finish_tile(qi)

        pl.when(diag)(lambda: run(True))
        pl.when(jnp.logical_not(diag))(lambda: run(False))
        return qi_n, j_n

    reset_stats()
    for h in range(FA_NH):
        scores(0, 0, sa_ref, h)

    def pair_body(i, carry):
        qi, j = carry
        qi, j = item(qi, j, sa_ref, sb_ref)
        qi, j = item(qi, j, sb_ref, sa_ref)
        return qi, j

    assert n_items % 2 == 0
    lax.fori_loop(0, n_items // 2, pair_body, (jnp.int32(0), jnp.int32(0)))


def _fox_attn_call(qa, ka, vt, og):
    bsz, _, seq, _ = qa.shape
    gw = FA_NP * LANES
    return pl.pallas_call(
        _fox_attn_kernel,
        grid=(bsz, FOX_HEADS // FA_NH),
        in_specs=[
            pl.BlockSpec((None, FA_NH, seq, LANES), lambda b, p: (b, p, 0, 0)),
            pl.BlockSpec((None, FA_NH, seq, LANES), lambda b, p: (b, p, 0, 0)),
            pl.BlockSpec((None, FA_NH, seq // FA_T, LANES, FA_T), lambda b, p: (b, p, 0, 0, 0)),
            pl.BlockSpec((None, seq, gw), lambda b, p: (b, 0, p)),
        ],
        out_specs=pl.BlockSpec((None, seq, gw), lambda b, p: (b, 0, p)),
        out_shape=jax.ShapeDtypeStruct((bsz, seq, D_MODEL), BF16),
        scratch_shapes=[
            pltpu.VMEM((FA_NH, 1, FA_T), F32),
            pltpu.VMEM((FA_NH, LANES, FA_T), F32),
            pltpu.VMEM((FA_NH, FA_T, FA_T), F32),
            pltpu.VMEM((FA_NH, FA_T, FA_T), F32),
        ],
        compiler_params=_cparams(("arbitrary", "arbitrary")),
        name="fox_attn",
    )(qa, ka, vt, og)


def _pad_cols(w, n):
    return jnp.pad(w, ((0, 0), (0, n - w.shape[1])))


def _gate_weights(w_a, w_x):
    per_half = RG_BLOCKS // 2
    half = RG_WIDTH // 2

    def bd(w, j):
        m = jnp.zeros((half, half), F32)
        for i in range(per_half):
            m = lax.dynamic_update_slice(m, w[j * per_half + i], (i * RG_BLOCK, i * RG_BLOCK))
        return m

    return jnp.stack([jnp.concatenate([bd(w_a, j), bd(w_x, j)], axis=1) for j in range(2)]).astype(BF16)


def kernel(x, c, ada_w, ada_b, ln_g, ln_b, ev_w_in, ev_conv_w, ev_conv_b, ev_rg_wa, ev_rg_ba, ev_rg_wx,
           ev_rg_bx, ev_rg_lam, ev_gla_w_up, ev_gla_b_up, ev_gla_norm_g, ev_w_out, od_w_in, od_b_f,
           od_q_norm_g, od_k_norm_g, od_w_out, mlp_w1, mlp_b1, mlp_w2, mlp_b2):
    bsz = x.shape[0]
    mods = _ada_call(c, ada_w.reshape(2 * DEPTH, D_MODEL, 3 * D_MODEL),
                     ada_b.reshape(2 * DEPTH, 1, 3 * D_MODEL))
    mods = mods.reshape(2 * DEPTH, bsz, 1, 3 * D_MODEL)
    pq, pk = _fox_place_matrices()
    pq = jnp.asarray(pq, BF16)
    pk = jnp.asarray(pk, BF16)

    for layer in range(DEPTH):
        mod_mix = mods[2 * layer]
        mod_mlp = mods[2 * layer + 1]
        mlp_params = [mlp_w1[layer].astype(BF16), mlp_b1[layer][None], mlp_w2[layer].astype(BF16),
                      mlp_b2[layer][None], ln_g[layer, 1][None], ln_b[layer, 1][None]]
        if layer % 2 == 0:
            e = layer // 2
            w_up = jnp.pad(ev_gla_w_up[e], ((0, LANES - GLA_LOWRANK), (0, 0)))
            mixer_params = [
                _pad_cols(ev_w_in[e], EVEN_IN_PAD).astype(BF16), ev_conv_w[e], ev_conv_b[e][None],
                _gate_weights(ev_rg_wa[e], ev_rg_wx[e]), ev_rg_ba[e][None], ev_rg_bx[e][None],
                ev_rg_lam[e][None], w_up, ev_gla_b_up[e][None], ev_gla_norm_g[e][None],
                ev_w_out[e].astype(BF16), ln_g[layer, 0][None], ln_b[layer, 0][None]]
            x = _even_layer_call(x, mod_mix, mixer_params, mod_mlp, mlp_params)
        else:
            o = layer // 2
            w_in = _pad_cols(od_w_in[o], ODD_IN_PAD).astype(BF16)
            b_f = jnp.pad(od_b_f[o], (0, LANES - FOX_HEADS))[None]
            gq = jnp.tile(od_q_norm_g[o], 2)[None]
            gk = jnp.tile(od_k_norm_g[o], 2)[None]
            qa, ka, vt, og = _fox_prep_call(x, mod_mix, w_in, b_f, gq, gk, pq, pk)
            mix = _fox_attn_call(qa, ka, vt, og)
            x = _post_call(mix, od_w_out[o].astype(BF16), x, mod_mix, ln_g[layer, 0][None],
                           ln_b[layer, 0][None], mod_mlp, mlp_params)
    return x
```

```python
import functools

import numpy as np
import jax
import jax.numpy as jnp
from jax import lax
from jax.experimental import pallas as pl
from jax.experimental.pallas import tpu as pltpu

F32 = jnp.float32
BF16 = jnp.bfloat16

D_MODEL = 1024
DEPTH = 2
RG_WIDTH = D_MODEL // 2
RG_BLOCKS = 8
RG_BLOCK = RG_WIDTH // RG_BLOCKS
CONV_WIDTH = 4
RG_C = 8.0
GLA_HEADS = 4
GLA_VDIM = D_MODEL // 2
GLA_KDIM = GLA_VDIM // 2
GLA_HK = GLA_KDIM // GLA_HEADS
GLA_HV = GLA_VDIM // GLA_HEADS
GLA_LOWRANK = 16
GLA_GATE_TAU = 16.0
GLA_CHUNK = 64
EVEN_IN = 2 * RG_WIDTH + 2 * GLA_KDIM + 2 * GLA_VDIM + GLA_LOWRANK
FOX_HEADS = 16
FOX_HD = D_MODEL // FOX_HEADS
FOX_PAIRS = FOX_HEADS // 2
ODD_IN = 4 * D_MODEL + FOX_HEADS
D_FF = 4 * D_MODEL
ALPHA = (2 * DEPTH) ** 0.25
LN_EPS = 1e-5
RMS_EPS = 1e-6

LANES = 128
SUBLANES = 8
VMEM_LIMIT = 56 * 1024 * 1024

EVEN_IN_PAD = EVEN_IN + (LANES - GLA_LOWRANK)
ODD_IN_PAD = ODD_IN + (LANES - FOX_HEADS)
MASK_NEG = -1e30
LOG2E = float(np.log2(np.e))

F_PARTS = 3
F_SLOTS = 8

PROJ_TN = 512


def _cparams(sem):
    return pltpu.CompilerParams(dimension_semantics=sem, vmem_limit_bytes=VMEM_LIMIT)


def _resident(shape):
    nd = len(shape)
    return pl.BlockSpec(shape, lambda *_: (0,) * nd, pipeline_mode=pl.Buffered(1))


def _sigmoid(x):
    return 0.5 * jnp.tanh(0.5 * x) + 0.5


def _log_sigmoid(x):
    return jnp.minimum(x, 0.0) - jnp.log1p(jnp.exp(-jnp.abs(x)))


def _gelu_tanh(x):
    c = np.sqrt(2.0 / np.pi).astype(np.float32)
    return x * (0.5 * (1.0 + jnp.tanh(c * (x + 0.044715 * (x * x * x)))))


def _sqrt_nonneg(y):
    return jnp.where(y > 0.0, y * lax.rsqrt(y), 0.0)


def _bdot(a, b):
    return jnp.dot(a, b, preferred_element_type=F32)


def _split2(a):
    hi = a.astype(BF16)
    lo = (a - hi.astype(F32)).astype(BF16)
    return hi, lo


def _dot_split(a, w):
    ah, al = _split2(a)
    wh, wl = _split2(w)
    return _bdot(ah, wh) + (_bdot(ah, wl) + _bdot(al, wh))


def _layer_norm(z, g, b):
    mu = jnp.mean(z, axis=-1, keepdims=True)
    zc = z - mu
    var = jnp.mean(zc * zc, axis=-1, keepdims=True)
    return zc * lax.rsqrt(var + LN_EPS) * g + b


def _modulate(x, mod):
    shift = mod[:, 0:D_MODEL]
    scale = mod[:, D_MODEL:2 * D_MODEL]
    return x * (1.0 + scale) + shift


def _gate(mod):
    return 1.0 + mod[:, 2 * D_MODEL:3 * D_MODEL]


MLP_TF = 1024


def _mlp_ln(x, mod, w1_ref, b1_ref, w2_ref, b2_ref, g_ref, b_ref):
    u = _modulate(x, mod).astype(BF16)
    y = jnp.zeros(x.shape, F32) + b2_ref[...]
    for f0 in range(0, D_FF, MLP_TF):
        hdn = _bdot(u, w1_ref[:, f0:f0 + MLP_TF]) + b1_ref[:, f0:f0 + MLP_TF]
        hdn = jnp.maximum(hdn, 0.0)
        y = y + _bdot((hdn * hdn).astype(BF16), w2_ref[f0:f0 + MLP_TF, :])
    z = ALPHA * x + _gate(mod) * y
    return _layer_norm(z, g_ref[...], b_ref[...])


ADA_TN = 1024


def _ada_kernel(c_ref, w_ref, b_ref, o_ref):
    c = c_ref[...]
    s = c * _sigmoid(c)
    o_ref[...] = _dot_split(s, w_ref[...]) + b_ref[...]


def _ada_call(c, ada_w, ada_b):
    n_mod = ada_w.shape[0]
    bsz = c.shape[0]
    return pl.pallas_call(
        _ada_kernel,
        grid=(n_mod, 3 * D_MODEL // ADA_TN),
        in_specs=[
            pl.BlockSpec((bsz, D_MODEL), lambda i, j: (0, 0)),
            pl.BlockSpec((None, D_MODEL, ADA_TN), lambda i, j: (i, 0, j)),
            pl.BlockSpec((None, 1, ADA_TN), lambda i, j: (i, 0, j)),
        ],
        out_specs=pl.BlockSpec((None, bsz, ADA_TN), lambda i, j: (i, 0, j)),
        out_shape=jax.ShapeDtypeStruct((n_mod, bsz, 3 * D_MODEL), F32),
        compiler_params=_cparams(("arbitrary", "arbitrary")),
        name="ada",
    )(c, ada_w, ada_b)


EV_T = 512
EV_CHUNKS = EV_T // GLA_CHUNK
EV_UP = 512
EV_DOWN = 256
_XR, _YR = 0, RG_WIDTH
_Q = 2 * RG_WIDTH
_K = _Q + GLA_KDIM
_V = _K + GLA_KDIM
_G = _V + GLA_VDIM
_ZL = _G + GLA_VDIM


def _scan_rows8(a, u):
    rows, ch = a.shape
    a3 = a.reshape(rows // SUBLANES, SUBLANES, ch)
    u3 = u.reshape(rows // SUBLANES, SUBLANES, ch)
    sub = lax.broadcasted_iota(jnp.int32, a3.shape, 1)
    d = 1
    while d < SUBLANES:
        keep = sub >= d
        u3 = jnp.where(keep, a3 * pltpu.roll(u3, d, 1) + u3, u3)
        a3 = jnp.where(keep, a3 * pltpu.roll(a3, d, 1), a3)
        d *= 2
    return a3, u3


def _even_layer_kernel(x_ref, mod_ref, win_ref, convw_ref, convb_ref, wg_ref, ba_ref, bx_ref, lam_ref,
                       wup_ref, bup_ref, gn_ref, wout_ref, lng_ref, lnb_ref,
                       mod2_ref, w1_ref, b1_ref, w2_ref, b2_ref, lng2_ref, lnb2_ref, o_ref,
                       xmid_ref, hid_ref, zmlp_ref, p_ref, mix_ref, tail_ref, hc_ref, st_ref, qd_ref, kd_ref, ke_ref, v_ref,
                       bc_ref, oo_ref, *, tiles_per_seq, n_tiles):
    step = pl.program_id(0)
    t_idx = jnp.minimum(step, n_tiles - 1) % tiles_per_seq

    @pl.when(step == 0)
    def _():
        xmid_ref[...] = jnp.zeros_like(xmid_ref)

    @pl.when(t_idx == 0)
    def _():
        tail_ref[...] = jnp.zeros_like(tail_ref)
        hc_ref[...] = jnp.zeros_like(hc_ref)
        st_ref[...] = jnp.zeros_like(st_ref)

    x_prev = _layer_norm(xmid_ref[...], lng_ref[...], lnb_ref[...])
    mod2 = mod2_ref[...]
    u_mlp = _modulate(x_prev, mod2).astype(BF16)

    def mlp_up(f):
        f0 = f * EV_UP
        hdn = _bdot(u_mlp, w1_ref[:, f0:f0 + EV_UP]) + b1_ref[:, f0:f0 + EV_UP]
        hdn = jnp.maximum(hdn, 0.0)
        hid_ref[:, f0:f0 + EV_UP] = (hdn * hdn).astype(BF16)

    def mlp_down(n):
        n0 = n * EV_DOWN
        y = _bdot(hid_ref[...], w2_ref[:, n0:n0 + EV_DOWN]) + b2_ref[:, n0:n0 + EV_DOWN]
        zmlp_ref[:, n0:n0 + EV_DOWN] = ALPHA * x_prev[:, n0:n0 + EV_DOWN] + _gate(mod2)[:, n0:n0 + EV_DOWN] * y

    u_in = _modulate(x_ref[...], mod_ref[...]).astype(BF16)
    for c0 in range(0, EVEN_IN_PAD, PROJ_TN):
        c1 = min(c0 + PROJ_TN, EVEN_IN_PAD)
        p_ref[:, c0:c1] = _bdot(u_in, win_ref[:, c0:c1])

    xr = p_ref[:, _XR:_XR + RG_WIDTH]
    tail = tail_ref[...]
    row8 = lax.broadcasted_iota(jnp.int32, (SUBLANES, RG_WIDTH), 0)
    xc = convb_ref[...] + convw_ref[CONV_WIDTH - 1:CONV_WIDTH, :] * xr
    for j in range(1, CONV_WIDTH):
        xs = pltpu.roll(xr, j, 0)
        head = jnp.where(row8 < j, pltpu.roll(tail, j, 0), xs[0:SUBLANES])
        xs = jnp.concatenate([head, xs[SUBLANES:]], axis=0)
        xc = xc + convw_ref[CONV_WIDTH - 1 - j:CONV_WIDTH - j, :] * xs
    tail_ref[...] = xr[EV_T - SUBLANES:EV_T]

    half = RG_WIDTH // 2
    r_parts, i_parts = [], []
    for j in range(2):
        ri = _bdot(xc[:, j * half:(j + 1) * half].astype(BF16), wg_ref[j])
        r_parts.append(ri[:, :half])
        i_parts.append(ri[:, half:])
    for f in range(3):
        mlp_up(f)
    r = _sigmoid(jnp.concatenate(r_parts, axis=1) + ba_ref[...])
    ig = _sigmoid(jnp.concatenate(i_parts, axis=1) + bx_ref[...])
    log_a = RG_C * r * _log_sigmoid(lam_ref[...])
    a = jnp.exp(log_a)
    u = _sqrt_nonneg(-jnp.tanh(log_a) * (a * a + 1.0)) * (ig * xc)

    a3, h3 = _scan_rows8(a, u)
    carry = jnp.broadcast_to(hc_ref[SUBLANES - 1:SUBLANES, :], (SUBLANES, RG_WIDTH))
    h_groups = []
    n_groups = EV_T // SUBLANES
    for g in range(n_groups):
        hg = h3[g] + a3[g] * carry
        h_groups.append(hg)
        carry = jnp.broadcast_to(hg[SUBLANES - 1:SUBLANES, :], (SUBLANES, RG_WIDTH))
        if g == n_groups // 3:
            mlp_up(3)
        if g == 2 * n_groups // 3:
            mlp_up(4)
    hc_ref[...] = carry
    h = jnp.concatenate(h_groups, axis=0)
    mix_ref[:, 0:RG_WIDTH] = (h * _gelu_tanh(p_ref[:, _YR:_YR + RG_WIDTH])).astype(BF16)
    mlp_up(5)

    z = _dot_split(p_ref[:, _ZL:_ZL + LANES], wup_ref[...]) + bup_ref[...]
    bc = _log_sigmoid(z) * (1.0 / GLA_GATE_TAU)
    rowk = lax.broadcasted_iota(jnp.int32, (EV_T, GLA_KDIM), 0) & (GLA_CHUNK - 1)
    d = 1
    while d < GLA_CHUNK:
        bc = jnp.where(rowk >= d, bc + pltpu.roll(bc, d, 0), bc)
        d *= 2
    bc3 = bc.reshape(EV_CHUNKS, GLA_CHUNK, GLA_KDIM)
    b_last = jnp.broadcast_to(bc3[:, GLA_CHUNK - 1:GLA_CHUNK, :], bc3.shape).reshape(EV_T, GLA_KDIM)
    q = p_ref[:, _Q:_Q + GLA_KDIM]
    k = p_ref[:, _K:_K + GLA_KDIM]
    qd_ref[...] = (q * (GLA_HK ** -0.5) * jnp.exp(bc)).astype(BF16)
    kd_ref[...] = (k * jnp.exp(-bc)).astype(BF16)
    ke_ref[...] = (k * jnp.exp(b_last - bc)).astype(BF16)
    v_ref[...] = p_ref[:, _V:_V + GLA_VDIM].astype(BF16)
    bc_ref[...] = bc

    lane_k = lax.broadcasted_iota(jnp.int32, (GLA_CHUNK, GLA_KDIM), 1) // GLA_HK
    lane_v = lax.broadcasted_iota(jnp.int32, (GLA_CHUNK, GLA_VDIM), 1) // GLA_HV
    tri_r = lax.broadcasted_iota(jnp.int32, (GLA_CHUNK, GLA_HEADS * GLA_CHUNK), 0)
    tri_c = lax.broadcasted_iota(jnp.int32, (GLA_CHUNK, GLA_HEADS * GLA_CHUNK), 1) & (GLA_CHUNK - 1)
    st_r = lax.broadcasted_iota(jnp.int32, (GLA_VDIM, GLA_KDIM), 0) // GLA_HV
    st_c = lax.broadcasted_iota(jnp.int32, (GLA_VDIM, GLA_KDIM), 1) // GLA_HK
    nt = (((1,), (1,)), ((), ()))
    tn = (((0,), (0,)), ((), ()))

    def chunk_body(c):
        r0 = c * GLA_CHUNK
        qd = qd_ref[r0:r0 + GLA_CHUNK, :]
        kd = kd_ref[r0:r0 + GLA_CHUNK, :]
        ke = ke_ref[r0:r0 + GLA_CHUNK, :]
        vv = v_ref[r0:r0 + GLA_CHUNK, :]
        zk = jnp.zeros_like(kd)
        zv = jnp.zeros_like(vv)
        kbd = jnp.concatenate([jnp.where(lane_k == hh, kd, zk) for hh in range(GLA_HEADS)], axis=0)
        att = lax.dot_general(qd, kbd, nt, preferred_element_type=F32)
        att = jnp.where(tri_c <= tri_r, att, 0.0).astype(BF16)
        vbd = jnp.concatenate([jnp.where(lane_v == hh, vv, zv) for hh in range(GLA_HEADS)], axis=0)
        st = st_ref[...]
        o = _bdot(att, vbd) + lax.dot_general(qd, st.astype(BF16), nt, preferred_element_type=F32)
        oo_ref[r0:r0 + GLA_CHUNK, :] = o
        kv_t = lax.dot_general(vv, ke, tn, preferred_element_type=F32)
        decay = jnp.exp(bc_ref[r0 + GLA_CHUNK - 1:r0 + GLA_CHUNK, :])
        st_ref[...] = decay * st + jnp.where(st_r == st_c, kv_t, 0.0)

    mlp_up(6)
    mlp_up(7)
    for c in range(EV_CHUNKS):
        chunk_body(c)
        if c % 2 == 1:
            mlp_down(c // 2)

    for hh in range(GLA_HEADS):
        oh = oo_ref[:, hh * GLA_HV:(hh + 1) * GLA_HV]
        ms = jnp.mean(oh * oh, axis=-1, keepdims=True)
        gg = p_ref[:, _G + hh * GLA_HV:_G + (hh + 1) * GLA_HV]
        on = oh * lax.rsqrt(ms + RMS_EPS) * gn_ref[:, hh * GLA_HV:(hh + 1) * GLA_HV]
        mix_ref[:, RG_WIDTH + hh * GLA_HV:RG_WIDTH + (hh + 1) * GLA_HV] = (on * (gg * _sigmoid(gg))).astype(BF16)

    y = _bdot(mix_ref[...], wout_ref[...])
    o_ref[...] = _layer_norm(zmlp_ref[...], lng2_ref[...], lnb2_ref[...])
    xmid_ref[...] = ALPHA * x_ref[...] + _gate(mod_ref[...]) * y


def _even_layer_call(x, mod, mixer_params, mod_mlp, mlp_params):
    bsz, seq, _ = x.shape
    tiles_per_seq = seq // EV_T
    n_tiles = bsz * tiles_per_seq

    def mix_tile(s):
        return jnp.minimum(s, n_tiles - 1)

    def mlp_tile(s):
        return jnp.maximum(s - 1, 0)

    def tile_spec(tile_of):
        return pl.BlockSpec((None, EV_T, D_MODEL),
                            lambda s: (tile_of(s) // tiles_per_seq, tile_of(s) % tiles_per_seq, 0))

    def mod_spec(tile_of):
        return pl.BlockSpec((None, 1, 3 * D_MODEL), lambda s: (tile_of(s) // tiles_per_seq, 0, 0))

    return pl.pallas_call(
        functools.partial(_even_layer_kernel, tiles_per_seq=tiles_per_seq, n_tiles=n_tiles),
        grid=(n_tiles + 1,),
        in_specs=[tile_spec(mix_tile), mod_spec(mix_tile)]
        + [_resident(a.shape) for a in mixer_params] + [mod_spec(mlp_tile)]
        + [_resident(a.shape) for a in mlp_params],
        out_specs=tile_spec(mlp_tile),
        out_shape=jax.ShapeDtypeStruct((bsz, seq, D_MODEL), F32),
        scratch_shapes=[
            pltpu.VMEM((EV_T, D_MODEL), F32),
            pltpu.VMEM((EV_T, D_FF), BF16),
            pltpu.VMEM((EV_T, D_MODEL), F32),
            pltpu.VMEM((EV_T, EVEN_IN_PAD), F32),
            pltpu.VMEM((EV_T, D_MODEL), BF16),
            pltpu.VMEM((SUBLANES, RG_WIDTH), F32),
            pltpu.VMEM((SUBLANES, RG_WIDTH), F32),
            pltpu.VMEM((GLA_VDIM, GLA_KDIM), F32),
            pltpu.VMEM((EV_T, GLA_KDIM), BF16),
            pltpu.VMEM((EV_T, GLA_KDIM), BF16),
            pltpu.VMEM((EV_T, GLA_KDIM), BF16),
            pltpu.VMEM((EV_T, GLA_VDIM), BF16),
            pltpu.VMEM((EV_T, GLA_KDIM), F32),
            pltpu.VMEM((EV_T, GLA_VDIM), F32),
        ],
        compiler_params=_cparams(("arbitrary",)),
        name="even_layer",
    )(x, mod, *mixer_params, mod_mlp, *mlp_params)


POST_TM = 512


def _post_kernel(a_ref, wout_ref, x_ref, mod_ref, lng_ref, lnb_ref,
                 mod2_ref, w1_ref, b1_ref, w2_ref, b2_ref, lng2_ref, lnb2_ref, o_ref):
    y = _bdot(a_ref[...], wout_ref[...])
    z = ALPHA * x_ref[...] + _gate(mod_ref[...]) * y
    x_mid = _layer_norm(z, lng_ref[...], lnb_ref[...])
    o_ref[...] = _mlp_ln(x_mid, mod2_ref[...], w1_ref, b1_ref, w2_ref, b2_ref, lng2_ref, lnb2_ref)


def _post_call(a, w_out, x, mod, ln_g, ln_b, mod_mlp, mlp_params):
    bsz, seq, _ = x.shape
    tile = pl.BlockSpec((None, POST_TM, D_MODEL), lambda b, t: (b, t, 0))
    mod_spec = pl.BlockSpec((None, 1, 3 * D_MODEL), lambda b, t: (b, 0, 0))
    return pl.pallas_call(
        _post_kernel,
        grid=(bsz, seq // POST_TM),
        in_specs=[tile, _resident(w_out.shape), tile, mod_spec, _resident(ln_g.shape), _resident(ln_b.shape),
                  mod_spec] + [_resident(p.shape) for p in mlp_params],
        out_specs=tile,
        out_shape=jax.ShapeDtypeStruct((bsz, seq, D_MODEL), F32),
        compiler_params=_cparams(("arbitrary", "arbitrary")),
        name="post",
    )(a, w_out, x, mod, ln_g, ln_b, mod_mlp, *mlp_params)


FP_T = 512
FP_GW = 256
FP_GH = FP_GW // FOX_HD
_OQ, _OK, _OV, _OG, _OF = 0, D_MODEL, 2 * D_MODEL, 3 * D_MODEL, 4 * D_MODEL
F_ONE_COL = F_PARTS * FOX_HEADS


def _fox_place_matrices():
    pq = np.zeros((LANES, FOX_HEADS * LANES), np.float32)
    pk = np.zeros((LANES, FOX_HEADS * LANES), np.float32)
    for h in range(FOX_HEADS):
        base = h * LANES + (FOX_HD if h % 2 == 0 else 0)
        for part in range(F_PARTS):
            pq[part * FOX_HEADS + h, base + part] = 1.0
            pq[F_ONE_COL, base + F_PARTS + part] = 1.0
            pk[F_ONE_COL, base + part] = 1.0
            pk[part * FOX_HEADS + h, base + F_PARTS + part] = -1.0
    return pq, pk


def _fox_prep_kernel(x_ref, mod_ref, win_ref, bf_ref, gq_ref, gk_ref, pq_ref, pk_ref,
                     qa_ref, ka_ref, vt_ref, og_ref, fc_ref):
    t_idx = pl.program_id(1)

    @pl.when(t_idx == 0)
    def _():
        fc_ref[...] = jnp.zeros_like(fc_ref)

    u_in = _modulate(x_ref[...], mod_ref[...]).astype(BF16)

    def proj(c0, width):
        return _bdot(u_in, win_ref[:, c0:c0 + width])

    lane = lax.broadcasted_iota(jnp.int32, (FP_T, LANES), 1)
    f = _log_sigmoid(proj(_OF, LANES) + bf_ref[...])
    row = lax.broadcasted_iota(jnp.int32, (FP_T, LANES), 0)
    d = 1
    while d < FP_T:
        f = jnp.where(row >= d, f + pltpu.roll(f, d, 0), f)
        d *= 2
    f = f + fc_ref[SUBLANES - 1:SUBLANES, :]
    fc_ref[...] = f[FP_T - SUBLANES:FP_T]
    f = jnp.where(lane < FOX_HEADS, f * LOG2E, 0.0)
    f1 = f.astype(BF16).astype(F32)
    r1 = f - f1
    f2 = r1.astype(BF16).astype(F32)
    f3 = (r1 - f2).astype(BF16).astype(F32)
    packed = f1 + pltpu.roll(f2, FOX_HEADS, 1) + pltpu.roll(f3, 2 * FOX_HEADS, 1)
    packed = jnp.where(lane == F_ONE_COL, 1.0, packed).astype(BF16)

    left = lane < FOX_HD

    def head_norm(xp, g):
        sq = xp * xp
        ms_l = jnp.sum(jnp.where(left, sq, 0.0), axis=-1, keepdims=True)
        ms_r = jnp.sum(jnp.where(left, 0.0, sq), axis=-1, keepdims=True)
        ms = jnp.where(left, ms_l, ms_r) * (1.0 / FOX_HD)
        return xp * lax.rsqrt(ms + RMS_EPS) * g

    def group_products(gi):
        c0 = gi * FP_GW
        h0 = c0 // FOX_HD
        return (proj(_OQ + c0, FP_GW), proj(_OK + c0, FP_GW), proj(_OV + c0, FP_GW), proj(_OG + c0, FP_GW),
                _bdot(packed, pq_ref[:, h0 * LANES:(h0 + FP_GH) * LANES]),
                _bdot(packed, pk_ref[:, h0 * LANES:(h0 + FP_GH) * LANES]))

    def group_finish(gi, prods):
        qg, kg, vg, gg, fq, fk = prods
        c0 = gi * FP_GW
        og_ref[:, c0:c0 + FP_GW] = _sigmoid(gg).astype(BF16)
        for pi in range(FP_GW // LANES):
            l0 = pi * LANES
            qn = head_norm(qg[:, l0:l0 + LANES], gq_ref[...]) * (FOX_HD ** -0.5 * LOG2E)
            kn = head_norm(kg[:, l0:l0 + LANES], gk_ref[...])
            vv = vg[:, l0:l0 + LANES]
            for e in range(2):
                hl = 2 * pi + e
                h = gi * FP_GH + hl
                own = left if e == 0 else jnp.logical_not(left)
                qa_ref[h] = jnp.where(own, qn, fq[:, hl * LANES:(hl + 1) * LANES]).astype(BF16)
                ka_ref[h] = jnp.where(own, kn, fk[:, hl * LANES:(hl + 1) * LANES]).astype(BF16)
                vt_ref[h, 0] = jnp.where(own, vv, 1.0).T.astype(BF16)

    n_groups = D_MODEL // FP_GW
    pending = group_products(0)
    for gi in range(n_groups):
        nxt = group_products(gi + 1) if gi + 1 < n_groups else None
        group_finish(gi, pending)
        pending = nxt


def _fox_prep_call(x, mod, w_in, b_f, gq, gk, pq, pk):
    bsz, seq, _ = x.shape
    small = [w_in, b_f, gq, gk, pq, pk]
    return pl.pallas_call(
        _fox_prep_kernel,
        grid=(bsz, seq // FP_T),
        in_specs=[pl.BlockSpec((None, FP_T, D_MODEL), lambda b, t: (b, t, 0)),
                  pl.BlockSpec((None, 1, 3 * D_MODEL), lambda b, t: (b, 0, 0))]
        + [_resident(a.shape) for a in small],
        out_specs=[
            pl.BlockSpec((None, FOX_HEADS, FP_T, LANES), lambda b, t: (b, 0, t, 0)),
            pl.BlockSpec((None, FOX_HEADS, FP_T, LANES), lambda b, t: (b, 0, t, 0)),
            pl.BlockSpec((None, FOX_HEADS, 1, LANES, FP_T), lambda b, t: (b, 0, t, 0, 0)),
            pl.BlockSpec((None, FP_T, D_MODEL), lambda b, t: (b, t, 0)),
        ],
        out_shape=[
            jax.ShapeDtypeStruct((bsz, FOX_HEADS, seq, LANES), BF16),
            jax.ShapeDtypeStruct((bsz, FOX_HEADS, seq, LANES), BF16),
            jax.ShapeDtypeStruct((bsz, FOX_HEADS, seq // FP_T, LANES, FP_T), BF16),
            jax.ShapeDtypeStruct((bsz, seq, D_MODEL), BF16),
        ],
        scratch_shapes=[pltpu.VMEM((SUBLANES, LANES), F32)],
        compiler_params=_cparams(("arbitrary", "arbitrary")),
        name="fox_prep",
    )(x, mod, *small)


FA_T = FP_T
FA_NH = 4
FA_NP = FA_NH // 2


def _fox_attn_kernel(qa_ref, ka_ref, vt_ref, g_ref, o_ref, m_ref, acc_ref, sa_ref, sb_ref):
    seq = qa_ref.shape[1]
    n_q = seq // FA_T
    n_items = n_q * (n_q + 1) // 2
    nt = (((1,), (1,)), ((), ()))
    key = lax.broadcasted_iota(jnp.int32, (FA_T, FA_T), 0)
    qry = lax.broadcasted_iota(jnp.int32, (FA_T, FA_T), 1)
    top = lax.broadcasted_iota(jnp.int32, (LANES, FA_T), 0) < FOX_HD

    def reset_stats():
        m_ref[...] = jnp.full_like(m_ref, MASK_NEG)
        acc_ref[...] = jnp.zeros_like(acc_ref)

    def scores(qi, j, dst_ref, h):
        q0 = pl.multiple_of(qi * FA_T, FA_T)
        k0 = pl.multiple_of(j * FA_T, FA_T)
        dst_ref[h] = lax.dot_general(ka_ref[h, pl.ds(k0, FA_T), :], qa_ref[h, pl.ds(q0, FA_T), :], nt,
                                     preferred_element_type=F32)

    def softmax_pv(j, src_ref, h, masked):
        s_t = src_ref[h]
        if masked:
            s_t = jnp.where(key <= qry, s_t, MASK_NEG)
        m_prev = m_ref[h]
        m_new = jnp.maximum(m_prev, jnp.max(s_t, axis=0, keepdims=True))
        p_t = jnp.exp2(s_t - m_new).astype(BF16)
        acc_ref[h] = jnp.exp2(m_prev - m_new) * acc_ref[h] + _bdot(vt_ref[h, j], p_t)
        m_ref[h] = m_new

    def finish_tile(qi):
        q0 = pl.multiple_of(qi * FA_T, FA_T)
        for p in range(FA_NP):
            acc0 = acc_ref[2 * p]
            acc1 = acc_ref[2 * p + 1]
            o_t = jnp.where(top, acc0 * (1.0 / acc0[FOX_HD:FOX_HD + 1, :]), acc1 * (1.0 / acc1[0:1, :]))
            g = g_ref[pl.ds(q0, FA_T), p * LANES:(p + 1) * LANES].astype(F32)
            o_ref[pl.ds(q0, FA_T), p * LANES:(p + 1) * LANES] = (o_t.T * g).astype(BF16)
        reset_stats()

    def item(qi, j, cur_ref, nxt_ref):
        diag = j == qi
        qi_n = jnp.where(diag, jnp.minimum(qi + 1, n_q - 1), qi)
        j_n = jnp.where(diag, 0, j + 1)

        def run(masked):
            scores(qi_n, j_n, nxt_ref, 0)
            for h in range(FA_NH):
                if h + 1 < FA_NH:
                    scores(qi_n, j_n, nxt_ref, h + 1)
                softmax_pv(j, cur_ref, h, masked)
            if masked:
                finish_tile(qi)

        pl.when(diag)(lambda: run(True))
        pl.when(jnp.logical_not(diag))(lambda: run(False))
        return qi_n, j_n

    reset_stats()
    for h in range(FA_NH):
        scores(0, 0, sa_ref, h)

    def pair_body(i, carry):
        qi, j = carry
        qi, j = item(qi, j, sa_ref, sb_ref)
        qi, j = item(qi, j, sb_ref, sa_ref)
        return qi, j

    assert n_items % 2 == 0
    lax.fori_loop(0, n_items // 2, pair_body, (jnp.int32(0), jnp.int32(0)))


def _fox_attn_call(qa, ka, vt, og):
    bsz, _, seq, _ = qa.shape
    gw = FA_NP * LANES
    return pl.pallas_call(
        _fox_attn_kernel,
        grid=(bsz, FOX_HEADS // FA_NH),
        in_specs=[
            pl.BlockSpec((None, FA_NH, seq, LANES), lambda b, p: (b, p, 0, 0)),
            pl.BlockSpec((None, FA_NH, seq, LANES), lambda b, p: (b, p, 0, 0)),
            pl.BlockSpec((None, FA_NH, seq // FA_T, LANES, FA_T), lambda b, p: (b, p, 0, 0, 0)),
            pl.BlockSpec((None, seq, gw), lambda b, p: (b, 0, p)),
        ],
        out_specs=pl.BlockSpec((None, seq, gw), lambda b, p: (b, 0, p)),
        out_shape=jax.ShapeDtypeStruct((bsz, seq, D_MODEL), BF16),
        scratch_shapes=[
            pltpu.VMEM((FA_NH, 1, FA_T), F32),
            pltpu.VMEM((FA_NH, LANES, FA_T), F32),
            pltpu.VMEM((FA_NH, FA_T, FA_T), F32),
            pltpu.VMEM((FA_NH, FA_T, FA_T), F32),
        ],
        compiler_params=_cparams(("arbitrary", "arbitrary")),
        name="fox_attn",
    )(qa, ka, vt, og)


def _pad_cols(w, n):
    return jnp.pad(w, ((0, 0), (0, n - w.shape[1])))


def _gate_weights(w_a, w_x):
    per_half = RG_BLOCKS // 2
    half = RG_WIDTH // 2

    def bd(w, j):
        m = jnp.zeros((half, half), F32)
        for i in range(per_half):
            m = lax.dynamic_update_slice(m, w[j * per_half + i], (i * RG_BLOCK, i * RG_BLOCK))
        return m

    return jnp.stack([jnp.concatenate([bd(w_a, j), bd(w_x, j)], axis=1) for j in range(2)]).astype(BF16)


def kernel(x, c, ada_w, ada_b, ln_g, ln_b, ev_w_in, ev_conv_w, ev_conv_b, ev_rg_wa, ev_rg_ba, ev_rg_wx,
           ev_rg_bx, ev_rg_lam, ev_gla_w_up, ev_gla_b_up, ev_gla_norm_g, ev_w_out, od_w_in, od_b_f,
           od_q_norm_g, od_k_norm_g, od_w_out, mlp_w1, mlp_b1, mlp_w2, mlp_b2):
    bsz = x.shape[0]
    mods = _ada_call(c, ada_w.reshape(2 * DEPTH, D_MODEL, 3 * D_MODEL),
                     ada_b.reshape(2 * DEPTH, 1, 3 * D_MODEL))
    mods = mods.reshape(2 * DEPTH, bsz, 1, 3 * D_MODEL)
    pq, pk = _fox_place_matrices()
    pq = jnp.asarray(pq, BF16)
    pk = jnp.asarray(pk, BF16)

    for layer in range(DEPTH):
        mod_mix = mods[2 * layer]
        mod_mlp = mods[2 * layer + 1]
        mlp_params = [mlp_w1[layer].astype(BF16), mlp_b1[layer][None], mlp_w2[layer].astype(BF16),
                      mlp_b2[layer][None], ln_g[layer, 1][None], ln_b[layer, 1][None]]
        if layer % 2 == 0:
            e = layer // 2
            w_up = jnp.pad(ev_gla_w_up[e], ((0, LANES - GLA_LOWRANK), (0, 0)))
            mixer_params = [
                _pad_cols(ev_w_in[e], EVEN_IN_PAD).astype(BF16), ev_conv_w[e], ev_conv_b[e][None],
                _gate_weights(ev_rg_wa[e], ev_rg_wx[e]), ev_rg_ba[e][None], ev_rg_bx[e][None],
                ev_rg_lam[e][None], w_up, ev_gla_b_up[e][None], ev_gla_norm_g[e][None],
                ev_w_out[e].astype(BF16), ln_g[layer, 0][None], ln_b[layer, 0][None]]
            x = _even_layer_call(x, mod_mix, mixer_params, mod_mlp, mlp_params)
        else:
            o = layer // 2
            w_in = _pad_cols(od_w_in[o], ODD_IN_PAD).astype(BF16)
            b_f = jnp.pad(od_b_f[o], (0, LANES - FOX_HEADS))[None]
            gq = jnp.tile(od_q_norm_g[o], 2)[None]
            gk = jnp.tile(od_k_norm_g[o], 2)[None]
            qa, ka, vt, og = _fox_prep_call(x, mod_mix, w_in, b_f, gq, gk, pq, pk)
            mix = _fox_attn_call(qa, ka, vt, og)
            x = _post_call(mix, od_w_out[o].astype(BF16), x, mod_mix, ln_g[layer, 0][None],
                           ln_b[layer, 0][None], mod_mlp, mlp_params)
    return x
```

```python
import functools

import numpy as np
import jax
import jax.numpy as jnp
from jax import lax
from jax.experimental import pallas as pl
from jax.experimental.pallas import tpu as pltpu

F32 = jnp.float32
BF16 = jnp.bfloat16

D_MODEL = 1024
DEPTH = 2
RG_WIDTH = D_MODEL // 2
RG_BLOCKS = 8
RG_BLOCK = RG_WIDTH // RG_BLOCKS
CONV_WIDTH = 4
RG_C = 8.0
GLA_HEADS = 4
GLA_VDIM = D_MODEL // 2
GLA_KDIM = GLA_VDIM // 2
GLA_HK = GLA_KDIM // GLA_HEADS
GLA_HV = GLA_VDIM // GLA_HEADS
GLA_LOWRANK = 16
GLA_GATE_TAU = 16.0
GLA_CHUNK = 64
EVEN_IN = 2 * RG_WIDTH + 2 * GLA_KDIM + 2 * GLA_VDIM + GLA_LOWRANK
FOX_HEADS = 16
FOX_HD = D_MODEL // FOX_HEADS
FOX_PAIRS = FOX_HEADS // 2
ODD_IN = 4 * D_MODEL + FOX_HEADS
D_FF = 4 * D_MODEL
ALPHA = (2 * DEPTH) ** 0.25
LN_EPS = 1e-5
RMS_EPS = 1e-6

LANES = 128
SUBLANES = 8
VMEM_LIMIT = 56 * 1024 * 1024

EVEN_IN_PAD = EVEN_IN + (LANES - GLA_LOWRANK)
ODD_IN_PAD = ODD_IN + (LANES - FOX_HEADS)
MASK_NEG = -1e30
LOG2E = float(np.log2(np.e))

F_PARTS = 3
F_SLOTS = 8

PROJ_TN = 512


def _cparams(sem):
    return pltpu.CompilerParams(dimension_semantics=sem, vmem_limit_bytes=VMEM_LIMIT)


def _resident(shape):
    nd = len(shape)
    return pl.BlockSpec(shape, lambda *_: (0,) * nd, pipeline_mode=pl.Buffered(1))


def _sigmoid(x):
    return 0.5 * jnp.tanh(0.5 * x) + 0.5


def _log_sigmoid(x):
    return jnp.minimum(x, 0.0) - jnp.log1p(jnp.exp(-jnp.abs(x)))


def _gelu_tanh(x):
    c = np.sqrt(2.0 / np.pi).astype(np.float32)
    return x * (0.5 * (1.0 + jnp.tanh(c * (x + 0.044715 * (x * x * x)))))


def _sqrt_nonneg(y):
    return jnp.where(y > 0.0, y * lax.rsqrt(y), 0.0)


def _bdot(a, b):
    return jnp.dot(a, b, preferred_element_type=F32)


def _split2(a):
    hi = a.astype(BF16)
    lo = (a - hi.astype(F32)).astype(BF16)
    return hi, lo


def _dot_split(a, w):
    ah, al = _split2(a)
    wh, wl = _split2(w)
    return _bdot(ah, wh) + (_bdot(ah, wl) + _bdot(al, wh))


def _layer_norm(z, g, b):
    mu = jnp.mean(z, axis=-1, keepdims=True)
    zc = z - mu
    var = jnp.mean(zc * zc, axis=-1, keepdims=True)
    return zc * lax.rsqrt(var + LN_EPS) * g + b


LN_ROWS = 64


def _ln_rows(src_ref, g_ref, b_ref, emit):
    g = g_ref[...]
    b = b_ref[...]
    for r0 in range(0, src_ref.shape[0], LN_ROWS):
        rows = slice(r0, r0 + LN_ROWS)
        emit(rows, _layer_norm(src_ref[rows, :], g, b))


def _modulate(x, mod):
    shift = mod[:, 0:D_MODEL]
    scale = mod[:, D_MODEL:2 * D_MODEL]
    return x * (1.0 + scale) + shift


def _gate(mod):
    return 1.0 + mod[:, 2 * D_MODEL:3 * D_MODEL]


MLP_UP = 512
MLP_DOWN = 256


def _mlp_pieces(src_ref, g_ref, b_ref, mod, w1_ref, b1_ref, w2_ref, b2_ref, xn_ref, u_ref, hid_ref, z_ref):
    gate = _gate(mod)

    def prepare():
        def emit(rows, xn):
            xn_ref[rows, :] = xn
            u_ref[rows, :] = _modulate(xn, mod).astype(BF16)
        _ln_rows(src_ref, g_ref, b_ref, emit)

    def up(f):
        f0 = f * MLP_UP
        hdn = _bdot(u_ref[...], w1_ref[:, f0:f0 + MLP_UP]) + b1_ref[:, f0:f0 + MLP_UP]
        hdn = jnp.maximum(hdn, 0.0)
        hid_ref[:, f0:f0 + MLP_UP] = (hdn * hdn).astype(BF16)

    def down(n):
        n0 = n * MLP_DOWN
        y = _bdot(hid_ref[...], w2_ref[:, n0:n0 + MLP_DOWN]) + b2_ref[:, n0:n0 + MLP_DOWN]
        z_ref[:, n0:n0 + MLP_DOWN] = ALPHA * xn_ref[:, n0:n0 + MLP_DOWN] + gate[:, n0:n0 + MLP_DOWN] * y

    return prepare, up, down


ADA_TN = 1024


def _ada_kernel(c_ref, w_ref, b_ref, o_ref):
    c = c_ref[...]
    s = c * _sigmoid(c)
    o_ref[...] = _dot_split(s, w_ref[...]) + b_ref[...]


def _ada_call(c, ada_w, ada_b):
    n_mod = ada_w.shape[0]
    bsz = c.shape[0]
    return pl.pallas_call(
        _ada_kernel,
        grid=(n_mod, 3 * D_MODEL // ADA_TN),
        in_specs=[
            pl.BlockSpec((bsz, D_MODEL), lambda i, j: (0, 0)),
            pl.BlockSpec((None, D_MODEL, ADA_TN), lambda i, j: (i, 0, j)),
            pl.BlockSpec((None, 1, ADA_TN), lambda i, j: (i, 0, j)),
        ],
        out_specs=pl.BlockSpec((None, bsz, ADA_TN), lambda i, j: (i, 0, j)),
        out_shape=jax.ShapeDtypeStruct((n_mod, bsz, 3 * D_MODEL), F32),
        compiler_params=_cparams(("arbitrary", "arbitrary")),
        name="ada",
    )(c, ada_w, ada_b)


EV_T = 512
EV_CHUNKS = EV_T // GLA_CHUNK
_XR, _YR = 0, RG_WIDTH
_Q = 2 * RG_WIDTH
_K = _Q + GLA_KDIM
_V = _K + GLA_KDIM
_G = _V + GLA_VDIM
_ZL = _G + GLA_VDIM


def _scan_rows8(a, u):
    rows, ch = a.shape
    a3 = a.reshape(rows // SUBLANES, SUBLANES, ch)
    u3 = u.reshape(rows // SUBLANES, SUBLANES, ch)
    sub = lax.broadcasted_iota(jnp.int32, a3.shape, 1)
    d = 1
    while d < SUBLANES:
        keep = sub >= d
        u3 = jnp.where(keep, a3 * pltpu.roll(u3, d, 1) + u3, u3)
        a3 = jnp.where(keep, a3 * pltpu.roll(a3, d, 1), a3)
        d *= 2
    return a3, u3


def _even_layer_kernel(x_ref, mod_ref, win_ref, convw_ref, convb_ref, wg_ref, ba_ref, bx_ref, lam_ref,
                       wup_ref, bup_ref, gn_ref, wout_ref, lng_ref, lnb_ref,
                       mod2_ref, w1_ref, b1_ref, w2_ref, b2_ref, lng2_ref, lnb2_ref, o_ref,
                       xmid_ref, xn_ref, umlp_ref, hid_ref, zmlp_ref, p_ref, mix_ref, tail_ref, hc_ref, st_ref,
                       qd_ref, kd_ref, ke_ref, v_ref, bc_ref, oo_ref, *, tiles_per_seq, n_tiles):
    step = pl.program_id(0)
    t_idx = jnp.minimum(step, n_tiles - 1) % tiles_per_seq

    @pl.when(step == 0)
    def _():
        xmid_ref[...] = jnp.zeros_like(xmid_ref)

    @pl.when(t_idx == 0)
    def _():
        tail_ref[...] = jnp.zeros_like(tail_ref)
        hc_ref[...] = jnp.zeros_like(hc_ref)
        st_ref[...] = jnp.zeros_like(st_ref)

    mlp_prepare, mlp_up, mlp_down = _mlp_pieces(xmid_ref, lng_ref, lnb_ref, mod2_ref[...], w1_ref, b1_ref,
                                                w2_ref, b2_ref, xn_ref, umlp_ref, hid_ref, zmlp_ref)
    mlp_prepare()

    u_in = _modulate(x_ref[...], mod_ref[...]).astype(BF16)
    for c0 in range(0, EVEN_IN_PAD, PROJ_TN):
        c1 = min(c0 + PROJ_TN, EVEN_IN_PAD)
        p_ref[:, c0:c1] = _bdot(u_in, win_ref[:, c0:c1])

    xr = p_ref[:, _XR:_XR + RG_WIDTH]
    tail = tail_ref[...]
    row8 = lax.broadcasted_iota(jnp.int32, (SUBLANES, RG_WIDTH), 0)
    xc = convb_ref[...] + convw_ref[CONV_WIDTH - 1:CONV_WIDTH, :] * xr
    for j in range(1, CONV_WIDTH):
        xs = pltpu.roll(xr, j, 0)
        head = jnp.where(row8 < j, pltpu.roll(tail, j, 0), xs[0:SUBLANES])
        xs = jnp.concatenate([head, xs[SUBLANES:]], axis=0)
        xc = xc + convw_ref[CONV_WIDTH - 1 - j:CONV_WIDTH - j, :] * xs
    tail_ref[...] = xr[EV_T - SUBLANES:EV_T]

    half = RG_WIDTH // 2
    r_parts, i_parts = [], []
    for j in range(2):
        ri = _bdot(xc[:, j * half:(j + 1) * half].astype(BF16), wg_ref[j])
        r_parts.append(ri[:, :half])
        i_parts.append(ri[:, half:])
    for f in range(3):
        mlp_up(f)
    r = _sigmoid(jnp.concatenate(r_parts, axis=1) + ba_ref[...])
    ig = _sigmoid(jnp.concatenate(i_parts, axis=1) + bx_ref[...])
    log_a = RG_C * r * _log_sigmoid(lam_ref[...])
    a = jnp.exp(log_a)
    u = _sqrt_nonneg(-jnp.tanh(log_a) * (a * a + 1.0)) * (ig * xc)

    a3, h3 = _scan_rows8(a, u)
    carry = jnp.broadcast_to(hc_ref[SUBLANES - 1:SUBLANES, :], (SUBLANES, RG_WIDTH))
    h_groups = []
    n_groups = EV_T // SUBLANES
    for g in range(n_groups):
        hg = h3[g] + a3[g] * carry
        h_groups.append(hg)
        carry = jnp.broadcast_to(hg[SUBLANES - 1:SUBLANES, :], (SUBLANES, RG_WIDTH))
        if g == n_groups // 3:
            mlp_up(3)
        if g == 2 * n_groups // 3:
            mlp_up(4)
    hc_ref[...] = carry
    h = jnp.concatenate(h_groups, axis=0)
    mix_ref[:, 0:RG_WIDTH] = (h * _gelu_tanh(p_ref[:, _YR:_YR + RG_WIDTH])).astype(BF16)
    mlp_up(5)

    z = _dot_split(p_ref[:, _ZL:_ZL + LANES], wup_ref[...]) + bup_ref[...]
    bc = _log_sigmoid(z) * (1.0 / GLA_GATE_TAU)
    rowk = lax.broadcasted_iota(jnp.int32, (EV_T, GLA_KDIM), 0) & (GLA_CHUNK - 1)
    d = 1
    while d < GLA_CHUNK:
        bc = jnp.where(rowk >= d, bc + pltpu.roll(bc, d, 0), bc)
        d *= 2
    bc3 = bc.reshape(EV_CHUNKS, GLA_CHUNK, GLA_KDIM)
    b_last = jnp.broadcast_to(bc3[:, GLA_CHUNK - 1:GLA_CHUNK, :], bc3.shape).reshape(EV_T, GLA_KDIM)
    q = p_ref[:, _Q:_Q + GLA_KDIM]
    k = p_ref[:, _K:_K + GLA_KDIM]
    qd_ref[...] = (q * (GLA_HK ** -0.5) * jnp.exp(bc)).astype(BF16)
    kd_ref[...] = (k * jnp.exp(-bc)).astype(BF16)
    ke_ref[...] = (k * jnp.exp(b_last - bc)).astype(BF16)
    v_ref[...] = p_ref[:, _V:_V + GLA_VDIM].astype(BF16)
    bc_ref[...] = bc

    lane_k = lax.broadcasted_iota(jnp.int32, (GLA_CHUNK, GLA_KDIM), 1) // GLA_HK
    lane_v = lax.broadcasted_iota(jnp.int32, (GLA_CHUNK, GLA_VDIM), 1) // GLA_HV
    tri_r = lax.broadcasted_iota(jnp.int32, (GLA_CHUNK, GLA_HEADS * GLA_CHUNK), 0)
    tri_c = lax.broadcasted_iota(jnp.int32, (GLA_CHUNK, GLA_HEADS * GLA_CHUNK), 1) & (GLA_CHUNK - 1)
    st_r = lax.broadcasted_iota(jnp.int32, (GLA_VDIM, GLA_KDIM), 0) // GLA_HV
    st_c = lax.broadcasted_iota(jnp.int32, (GLA_VDIM, GLA_KDIM), 1) // GLA_HK
    nt = (((1,), (1,)), ((), ()))
    tn = (((0,), (0,)), ((), ()))

    def chunk_body(c):
        r0 = c * GLA_CHUNK
        qd = qd_ref[r0:r0 + GLA_CHUNK, :]
        kd = kd_ref[r0:r0 + GLA_CHUNK, :]
        ke = ke_ref[r0:r0 + GLA_CHUNK, :]
        vv = v_ref[r0:r0 + GLA_CHUNK, :]
        zk = jnp.zeros_like(kd)
        zv = jnp.zeros_like(vv)
        kbd = jnp.concatenate([jnp.where(lane_k == hh, kd, zk) for hh in range(GLA_HEADS)], axis=0)
        att = lax.dot_general(qd, kbd, nt, preferred_element_type=F32)
        att = jnp.where(tri_c <= tri_r, att, 0.0).astype(BF16)
        vbd = jnp.concatenate([jnp.where(lane_v == hh, vv, zv) for hh in range(GLA_HEADS)], axis=0)
        st = st_ref[...]
        o = _bdot(att, vbd) + lax.dot_general(qd, st.astype(BF16), nt, preferred_element_type=F32)
        oo_ref[r0:r0 + GLA_CHUNK, :] = o
        kv_t = lax.dot_general(vv, ke, tn, preferred_element_type=F32)
        decay = jnp.exp(bc_ref[r0 + GLA_CHUNK - 1:r0 + GLA_CHUNK, :])
        st_ref[...] = decay * st + jnp.where(st_r == st_c, kv_t, 0.0)

    mlp_up(6)
    mlp_up(7)
    for c in range(EV_CHUNKS):
        chunk_body(c)
        if c % 2 == 1:
            mlp_down(c // 2)

    for hh in range(GLA_HEADS):
        oh = oo_ref[:, hh * GLA_HV:(hh + 1) * GLA_HV]
        ms = jnp.mean(oh * oh, axis=-1, keepdims=True)
        gg = p_ref[:, _G + hh * GLA_HV:_G + (hh + 1) * GLA_HV]
        on = oh * lax.rsqrt(ms + RMS_EPS) * gn_ref[:, hh * GLA_HV:(hh + 1) * GLA_HV]
        mix_ref[:, RG_WIDTH + hh * GLA_HV:RG_WIDTH + (hh + 1) * GLA_HV] = (on * (gg * _sigmoid(gg))).astype(BF16)

    y = _bdot(mix_ref[...], wout_ref[...])

    def emit_out(rows, xn):
        o_ref[rows, :] = xn
    _ln_rows(zmlp_ref, lng2_ref, lnb2_ref, emit_out)
    xmid_ref[...] = ALPHA * x_ref[...] + _gate(mod_ref[...]) * y


def _even_layer_call(x, mod, mixer_params, mod_mlp, mlp_params):
    bsz, seq, _ = x.shape
    tiles_per_seq = seq // EV_T
    n_tiles = bsz * tiles_per_seq

    def mix_tile(s):
        return jnp.minimum(s, n_tiles - 1)

    def mlp_tile(s):
        return jnp.maximum(s - 1, 0)

    def tile_spec(tile_of):
        return pl.BlockSpec((None, EV_T, D_MODEL),
                            lambda s: (tile_of(s) // tiles_per_seq, tile_of(s) % tiles_per_seq, 0))

    def mod_spec(tile_of):
        return pl.BlockSpec((None, 1, 3 * D_MODEL), lambda s: (tile_of(s) // tiles_per_seq, 0, 0))

    return pl.pallas_call(
        functools.partial(_even_layer_kernel, tiles_per_seq=tiles_per_seq, n_tiles=n_tiles),
        grid=(n_tiles + 1,),
        in_specs=[tile_spec(mix_tile), mod_spec(mix_tile)]
        + [_resident(a.shape) for a in mixer_params] + [mod_spec(mlp_tile)]
        + [_resident(a.shape) for a in mlp_params],
        out_specs=tile_spec(mlp_tile),
        out_shape=jax.ShapeDtypeStruct((bsz, seq, D_MODEL), F32),
        scratch_shapes=[
            pltpu.VMEM((EV_T, D_MODEL), F32),
            pltpu.VMEM((EV_T, D_MODEL), F32),
            pltpu.VMEM((EV_T, D_MODEL), BF16),
            pltpu.VMEM((EV_T, D_FF), BF16),
            pltpu.VMEM((EV_T, D_MODEL), F32),
            pltpu.VMEM((EV_T, EVEN_IN_PAD), F32),
            pltpu.VMEM((EV_T, D_MODEL), BF16),
            pltpu.VMEM((SUBLANES, RG_WIDTH), F32),
            pltpu.VMEM((SUBLANES, RG_WIDTH), F32),
            pltpu.VMEM((GLA_VDIM, GLA_KDIM), F32),
            pltpu.VMEM((EV_T, GLA_KDIM), BF16),
            pltpu.VMEM((EV_T, GLA_KDIM), BF16),
            pltpu.VMEM((EV_T, GLA_KDIM), BF16),
            pltpu.VMEM((EV_T, GLA_VDIM), BF16),
            pltpu.VMEM((EV_T, GLA_KDIM), F32),
            pltpu.VMEM((EV_T, GLA_VDIM), F32),
        ],
        compiler_params=_cparams(("arbitrary",)),
        name="even_layer",
    )(x, mod, *mixer_params, mod_mlp, *mlp_params)


POST_TM = 512


def _post_kernel(a_ref, wout_ref, x_ref, mod_ref, lng_ref, lnb_ref,
                 mod2_ref, w1_ref, b1_ref, w2_ref, b2_ref, lng2_ref, lnb2_ref, o_ref,
                 zmix_ref, xn_ref, umlp_ref, hid_ref, zmlp_ref):
    y = _bdot(a_ref[...], wout_ref[...])
    zmix_ref[...] = ALPHA * x_ref[...] + _gate(mod_ref[...]) * y
    mlp_prepare, mlp_up, mlp_down = _mlp_pieces(zmix_ref, lng_ref, lnb_ref, mod2_ref[...], w1_ref, b1_ref,
                                                w2_ref, b2_ref, xn_ref, umlp_ref, hid_ref, zmlp_ref)
    mlp_prepare()
    for f in range(D_FF // MLP_UP):
        mlp_up(f)
    for n in range(D_MODEL // MLP_DOWN):
        mlp_down(n)

    def emit_out(rows, xn):
        o_ref[rows, :] = xn
    _ln_rows(zmlp_ref, lng2_ref, lnb2_ref, emit_out)


def _post_call(a, w_out, x, mod, ln_g, ln_b, mod_mlp, mlp_params):
    bsz, seq, _ = x.shape
    tile = pl.BlockSpec((None, POST_TM, D_MODEL), lambda b, t: (b, t, 0))
    mod_spec = pl.BlockSpec((None, 1, 3 * D_MODEL), lambda b, t: (b, 0, 0))
    return pl.pallas_call(
        _post_kernel,
        grid=(bsz, seq // POST_TM),
        in_specs=[tile, _resident(w_out.shape), tile, mod_spec, _resident(ln_g.shape), _resident(ln_b.shape),
                  mod_spec] + [_resident(p.shape) for p in mlp_params],
        out_specs=tile,
        out_shape=jax.ShapeDtypeStruct((bsz, seq, D_MODEL), F32),
        scratch_shapes=[
            pltpu.VMEM((POST_TM, D_MODEL), F32),
            pltpu.VMEM((POST_TM, D_MODEL), F32),
            pltpu.VMEM((POST_TM, D_MODEL), BF16),
            pltpu.VMEM((POST_TM, D_FF), BF16),
            pltpu.VMEM((POST_TM, D_MODEL), F32),
        ],
        compiler_params=_cparams(("arbitrary", "arbitrary")),
        name="post",
    )(a, w_out, x, mod, ln_g, ln_b, mod_mlp, *mlp_params)


FP_T = 512
FP_GW = 256
FP_GH = FP_GW // FOX_HD
_OQ, _OK, _OV, _OG, _OF = 0, D_MODEL, 2 * D_MODEL, 3 * D_MODEL, 4 * D_MODEL
F_ONE_COL = F_PARTS * FOX_HEADS


def _fox_place_matrices():
    pq = np.zeros((LANES, FOX_HEADS * LANES), np.float32)
    pk = np.zeros((LANES, FOX_HEADS * LANES), np.float32)
    for h in range(FOX_HEADS):
        base = h * LANES + (FOX_HD if h % 2 == 0 else 0)
        for part in range(F_PARTS):
            pq[part * FOX_HEADS + h, base + part] = 1.0
            pq[F_ONE_COL, base + F_PARTS + part] = 1.0
            pk[F_ONE_COL, base + part] = 1.0
            pk[part * FOX_HEADS + h, base + F_PARTS + part] = -1.0
    return pq, pk


def _fox_prep_kernel(x_ref, mod_ref, win_ref, bf_ref, gq_ref, gk_ref, pq_ref, pk_ref,
                     qa_ref, ka_ref, vt_ref, og_ref, fc_ref):
    t_idx = pl.program_id(1)

    @pl.when(t_idx == 0)
    def _():
        fc_ref[...] = jnp.zeros_like(fc_ref)

    u_in = _modulate(x_ref[...], mod_ref[...]).astype(BF16)

    def proj(c0, width):
        return _bdot(u_in, win_ref[:, c0:c0 + width])

    lane = lax.broadcasted_iota(jnp.int32, (FP_T, LANES), 1)
    f = _log_sigmoid(proj(_OF, LANES) + bf_ref[...])
    row = lax.broadcasted_iota(jnp.int32, (FP_T, LANES), 0)
    d = 1
    while d < FP_T:
        f = jnp.where(row >= d, f + pltpu.roll(f, d, 0), f)
        d *= 2
    f = f + fc_ref[SUBLANES - 1:SUBLANES, :]
    fc_ref[...] = f[FP_T - SUBLANES:FP_T]
    f = jnp.where(lane < FOX_HEADS, f * LOG2E, 0.0)
    f1 = f.astype(BF16).astype(F32)
    r1 = f - f1
    f2 = r1.astype(BF16).astype(F32)
    f3 = (r1 - f2).astype(BF16).astype(F32)
    packed = f1 + pltpu.roll(f2, FOX_HEADS, 1) + pltpu.roll(f3, 2 * FOX_HEADS, 1)
    packed = jnp.where(lane == F_ONE_COL, 1.0, packed).astype(BF16)

    left = lane < FOX_HD

    def head_norm(xp, g):
        sq = xp * xp
        ms_l = jnp.sum(jnp.where(left, sq, 0.0), axis=-1, keepdims=True)
        ms_r = jnp.sum(jnp.where(left, 0.0, sq), axis=-1, keepdims=True)
        ms = jnp.where(left, ms_l, ms_r) * (1.0 / FOX_HD)
        return xp * lax.rsqrt(ms + RMS_EPS) * g

    def group_products(gi):
        c0 = gi * FP_GW
        h0 = c0 // FOX_HD
        return (proj(_OQ + c0, FP_GW), proj(_OK + c0, FP_GW), proj(_OV + c0, FP_GW), proj(_OG + c0, FP_GW),
                _bdot(packed, pq_ref[:, h0 * LANES:(h0 + FP_GH) * LANES]),
                _bdot(packed, pk_ref[:, h0 * LANES:(h0 + FP_GH) * LANES]))

    def group_finish(gi, prods):
        qg, kg, vg, gg, fq, fk = prods
        c0 = gi * FP_GW
        og_ref[:, c0:c0 + FP_GW] = _sigmoid(gg).astype(BF16)
        for pi in range(FP_GW // LANES):
            l0 = pi * LANES
            qn = head_norm(qg[:, l0:l0 + LANES], gq_ref[...]) * (FOX_HD ** -0.5 * LOG2E)
            kn = head_norm(kg[:, l0:l0 + LANES], gk_ref[...])
            vv = vg[:, l0:l0 + LANES]
            for e in range(2):
                hl = 2 * pi + e
                h = gi * FP_GH + hl
                own = left if e == 0 else jnp.logical_not(left)
                qa_ref[h] = jnp.where(own, qn, fq[:, hl * LANES:(hl + 1) * LANES]).astype(BF16)
                ka_ref[h] = jnp.where(own, kn, fk[:, hl * LANES:(hl + 1) * LANES]).astype(BF16)
                vt_ref[h, 0] = jnp.where(own, vv, 1.0).T.astype(BF16)

    n_groups = D_MODEL // FP_GW
    pending = group_products(0)
    for gi in range(n_groups):
        nxt = group_products(gi + 1) if gi + 1 < n_groups else None
        group_finish(gi, pending)
        pending = nxt


def _fox_prep_call(x, mod, w_in, b_f, gq, gk, pq, pk):
    bsz, seq, _ = x.shape
    small = [w_in, b_f, gq, gk, pq, pk]
    return pl.pallas_call(
        _fox_prep_kernel,
        grid=(bsz, seq // FP_T),
        in_specs=[pl.BlockSpec((None, FP_T, D_MODEL), lambda b, t: (b, t, 0)),
                  pl.BlockSpec((None, 1, 3 * D_MODEL), lambda b, t: (b, 0, 0))]
        + [_resident(a.shape) for a in small],
        out_specs=[
            pl.BlockSpec((None, FOX_HEADS, FP_T, LANES), lambda b, t: (b, 0, t, 0)),
            pl.BlockSpec((None, FOX_HEADS, FP_T, LANES), lambda b, t: (b, 0, t, 0)),
            pl.BlockSpec((None, FOX_HEADS, 1, LANES, FP_T), lambda b, t: (b, 0, t, 0, 0)),
            pl.BlockSpec((None, FP_T, D_MODEL), lambda b, t: (b, t, 0)),
        ],
        out_shape=[
            jax.ShapeDtypeStruct((bsz, FOX_HEADS, seq, LANES), BF16),
            jax.ShapeDtypeStruct((bsz, FOX_HEADS, seq, LANES), BF16),
            jax.ShapeDtypeStruct((bsz, FOX_HEADS, seq // FP_T, LANES, FP_T), BF16),
            jax.ShapeDtypeStruct((bsz, seq, D_MODEL), BF16),
        ],
        scratch_shapes=[pltpu.VMEM((SUBLANES, LANES), F32)],
        compiler_params=_cparams(("arbitrary", "arbitrary")),
        name="fox_prep",
    )(x, mod, *small)


FA_T = FP_T
FA_NH = 4
FA_NP = FA_NH // 2


def _fox_attn_kernel(qa_ref, ka_ref, vt_ref, g_ref, o_ref, m_ref, acc_ref, sa_ref, sb_ref):
    seq = qa_ref.shape[1]
    n_q = seq // FA_T
    n_items = n_q * (n_q + 1) // 2
    nt = (((1,), (1,)), ((), ()))
    key = lax.broadcasted_iota(jnp.int32, (FA_T, FA_T), 0)
    qry = lax.broadcasted_iota(jnp.int32, (FA_T, FA_T), 1)
    top = lax.broadcasted_iota(jnp.int32, (LANES, FA_T), 0) < FOX_HD

    def reset_stats():
        m_ref[...] = jnp.full_like(m_ref, MASK_NEG)
        acc_ref[...] = jnp.zeros_like(acc_ref)

    def scores(qi, j, dst_ref, h):
        q0 = pl.multiple_of(qi * FA_T, FA_T)
        k0 = pl.multiple_of(j * FA_T, FA_T)
        dst_ref[h] = lax.dot_general(ka_ref[h, pl.ds(k0, FA_T), :], qa_ref[h, pl.ds(q0, FA_T), :], nt,
                                     preferred_element_type=F32)

    def softmax_pv(j, src_ref, h, masked):
        s_t = src_ref[h]
        if masked:
            s_t = jnp.where(key <= qry, s_t, MASK_NEG)
        m_prev = m_ref[h]
        m_new = jnp.maximum(m_prev, jnp.max(s_t, axis=0, keepdims=True))
        p_t = jnp.exp2(s_t - m_new).astype(BF16)
        acc_ref[h] = jnp.exp2(m_prev - m_new) * acc_ref[h] + _bdot(vt_ref[h, j], p_t)
        m_ref[h] = m_new

    def finish_tile(qi):
        q0 = pl.multiple_of(qi * FA_T, FA_T)
        for p in range(FA_NP):
            acc0 = acc_ref[2 * p]
            acc1 = acc_ref[2 * p + 1]
            o_t = jnp.where(top, acc0 * (1.0 / acc0[FOX_HD:FOX_HD + 1, :]), acc1 * (1.0 / acc1[0:1, :]))
            g = g_ref[pl.ds(q0, FA_T), p * LANES:(p + 1) * LANES].astype(F32)
            o_ref[pl.ds(q0, FA_T), p * LANES:(p + 1) * LANES] = (o_t.T * g).astype(BF16)
        reset_stats()

    def advance(qi, j):
        diag = j == qi
        return diag, jnp.where(diag, jnp.minimum(qi + 1, n_q - 1), qi), jnp.where(diag, 0, j + 1)

    def item(qi, j, cur_ref, masked, nxt, nxt_ref):
        for h in range(FA_NH):
            scores(nxt[0], nxt[1], nxt_ref, h)
            softmax_pv(j, cur_ref, h, masked)
        if masked:
            finish_tile(qi)

    reset_stats()
    for h in range(FA_NH):
        scores(0, 0, sa_ref, h)

    def pair_body(i, carry):
        qa, ja = carry
        diag_a, qb, jb = advance(qa, ja)
        diag_b, qc, jc = advance(qb, jb)

        def run(masked_a, masked_b):
            item(qa, ja, sa_ref, masked_a, (qb, jb), sb_ref)
            item(qb, jb, sb_ref, masked_b, (qc, jc), sa_ref)

        not_a = jnp.logical_not(diag_a)
        not_b = jnp.logical_not(diag_b)
        pl.when(jnp.logical_and(diag_a, not_b))(lambda: run(True, False))
        pl.when(jnp.logical_and(not_a, diag_b))(lambda: run(False, True))
        pl.when(jnp.logical_and(not_a, not_b))(lambda: run(False, False))
        return qc, jc

    assert n_q >= 2 and n_items % 2 == 0
    lax.fori_loop(0, n_items // 2, pair_body, (jnp.int32(0), jnp.int32(0)))


def _fox_attn_call(qa, ka, vt, og):
    bsz, _, seq, _ = qa.shape
    gw = FA_NP * LANES
    return pl.pallas_call(
        _fox_attn_kernel,
        grid=(bsz, FOX_HEADS // FA_NH),
        in_specs=[
            pl.BlockSpec((None, FA_NH, seq, LANES), lambda b, p: (b, p, 0, 0)),
            pl.BlockSpec((None, FA_NH, seq, LANES), lambda b, p: (b, p, 0, 0)),
            pl.BlockSpec((None, FA_NH, seq // FA_T, LANES, FA_T), lambda b, p: (b, p, 0, 0, 0)),
            pl.BlockSpec((None, seq, gw), lambda b, p: (b, 0, p)),
        ],
        out_specs=pl.BlockSpec((None, seq, gw), lambda b, p: (b, 0, p)),
        out_shape=jax.ShapeDtypeStruct((bsz, seq, D_MODEL), BF16),
        scratch_shapes=[
            pltpu.VMEM((FA_NH, 1, FA_T), F32),
            pltpu.VMEM((FA_NH, LANES, FA_T), F32),
            pltpu.VMEM((FA_NH, FA_T, FA_T), F32),
            pltpu.VMEM((FA_NH, FA_T, FA_T), F32),
        ],
        compiler_params=_cparams(("arbitrary", "arbitrary")),
        name="fox_attn",
    )(qa, ka, vt, og)


def _pad_cols(w, n):
    return jnp.pad(w, ((0, 0), (0, n - w.shape[1])))


def _gate_weights(w_a, w_x):
    per_half = RG_BLOCKS // 2
    half = RG_WIDTH // 2

    def bd(w, j):
        m = jnp.zeros((half, half), F32)
        for i in range(per_half):
            m = lax.dynamic_update_slice(m, w[j * per_half + i], (i * RG_BLOCK, i * RG_BLOCK))
        return m

    return jnp.stack([jnp.concatenate([bd(w_a, j), bd(w_x, j)], axis=1) for j in range(2)]).astype(BF16)


def kernel(x, c, ada_w, ada_b, ln_g, ln_b, ev_w_in, ev_conv_w, ev_conv_b, ev_rg_wa, ev_rg_ba, ev_rg_wx,
           ev_rg_bx, ev_rg_lam, ev_gla_w_up, ev_gla_b_up, ev_gla_norm_g, ev_w_out, od_w_in, od_b_f,
           od_q_norm_g, od_k_norm_g, od_w_out, mlp_w1, mlp_b1, mlp_w2, mlp_b2):
    bsz = x.shape[0]
    mods = _ada_call(c, ada_w.reshape(2 * DEPTH, D_MODEL, 3 * D_MODEL),
                     ada_b.reshape(2 * DEPTH, 1, 3 * D_MODEL))
    mods = mods.reshape(2 * DEPTH, bsz, 1, 3 * D_MODEL)
    pq, pk = _fox_place_matrices()
    pq = jnp.asarray(pq, BF16)
    pk = jnp.asarray(pk, BF16)

    for layer in range(DEPTH):
        mod_mix = mods[2 * layer]
        mod_mlp = mods[2 * layer + 1]
        mlp_params = [mlp_w1[layer].astype(BF16), mlp_b1[layer][None], mlp_w2[layer].astype(BF16),
                      mlp_b2[layer][None], ln_g[layer, 1][None], ln_b[layer, 1][None]]
        if layer % 2 == 0:
            e = layer // 2
            w_up = jnp.pad(ev_gla_w_up[e], ((0, LANES - GLA_LOWRANK), (0, 0)))
            mixer_params = [
                _pad_cols(ev_w_in[e], EVEN_IN_PAD).astype(BF16), ev_conv_w[e], ev_conv_b[e][None],
                _gate_weights(ev_rg_wa[e], ev_rg_wx[e]), ev_rg_ba[e][None], ev_rg_bx[e][None],
                ev_rg_lam[e][None], w_up, ev_gla_b_up[e][None], ev_gla_norm_g[e][None],
                ev_w_out[e].astype(BF16), ln_g[layer, 0][None], ln_b[layer, 0][None]]
            x = _even_layer_call(x, mod_mix, mixer_params, mod_mlp, mlp_params)
        else:
            o = layer // 2
            w_in = _pad_cols(od_w_in[o], ODD_IN_PAD).astype(BF16)
            b_f = jnp.pad(od_b_f[o], (0, LANES - FOX_HEADS))[None]
            gq = jnp.tile(od_q_norm_g[o], 2)[None]
            gk = jnp.tile(od_k_norm_g[o], 2)[None]
            qa, ka, vt, og = _fox_prep_call(x, mod_mix, w_in, b_f, gq, gk, pq, pk)
            mix = _fox_attn_call(qa, ka, vt, og)
            x = _post_call(mix, od_w_out[o].astype(BF16), x, mod_mix, ln_g[layer, 0][None],
                           ln_b[layer, 0][None], mod_mlp, mlp_params)
    return x
```

```python
import functools
import itertools

import numpy as np
import jax
import jax.numpy as jnp
from jax import lax
from jax.experimental import pallas as pl
from jax.experimental.pallas import tpu as pltpu

F32 = jnp.float32
BF16 = jnp.bfloat16

D_MODEL = 1024
DEPTH = 2
RG_WIDTH = D_MODEL // 2
RG_BLOCKS = 8
RG_BLOCK = RG_WIDTH // RG_BLOCKS
CONV_WIDTH = 4
RG_C = 8.0
GLA_HEADS = 4
GLA_VDIM = D_MODEL // 2
GLA_KDIM = GLA_VDIM // 2
GLA_HK = GLA_KDIM // GLA_HEADS
GLA_HV = GLA_VDIM // GLA_HEADS
GLA_LOWRANK = 16
GLA_GATE_TAU = 16.0
GLA_CHUNK = 64
EVEN_IN = 2 * RG_WIDTH + 2 * GLA_KDIM + 2 * GLA_VDIM + GLA_LOWRANK
FOX_HEADS = 16
FOX_HD = D_MODEL // FOX_HEADS
FOX_PAIRS = FOX_HEADS // 2
ODD_IN = 4 * D_MODEL + FOX_HEADS
D_FF = 4 * D_MODEL
ALPHA = (2 * DEPTH) ** 0.25
LN_EPS = 1e-5
RMS_EPS = 1e-6

LANES = 128
SUBLANES = 8
VMEM_LIMIT = 56 * 1024 * 1024

EVEN_IN_PAD = EVEN_IN + (LANES - GLA_LOWRANK)
ODD_IN_PAD = ODD_IN + (LANES - FOX_HEADS)
MASK_NEG = -1e30
LOG2E = float(np.log2(np.e))

F_PARTS = 3
F_SLOTS = 8

PROJ_TN = 512


def _cparams(sem):
    return pltpu.CompilerParams(dimension_semantics=sem, vmem_limit_bytes=VMEM_LIMIT)


def _resident(shape):
    nd = len(shape)
    return pl.BlockSpec(shape, lambda *_: (0,) * nd, pipeline_mode=pl.Buffered(1))


def _sigmoid(x):
    return 0.5 * jnp.tanh(0.5 * x) + 0.5


def _log_sigmoid(x):
    return jnp.minimum(x, 0.0) - jnp.log1p(jnp.exp(-jnp.abs(x)))


def _gelu_tanh(x):
    c = np.sqrt(2.0 / np.pi).astype(np.float32)
    return x * (0.5 * (1.0 + jnp.tanh(c * (x + 0.044715 * (x * x * x)))))


def _sqrt_nonneg(y):
    return jnp.where(y > 0.0, y * lax.rsqrt(y), 0.0)


def _bdot(a, b):
    return jnp.dot(a, b, preferred_element_type=F32)


def _split2(a):
    hi = a.astype(BF16)
    lo = (a - hi.astype(F32)).astype(BF16)
    return hi, lo


def _dot_split(a, w):
    ah, al = _split2(a)
    wh, wl = _split2(w)
    return _bdot(ah, wh) + (_bdot(ah, wl) + _bdot(al, wh))


def _layer_norm(z, g, b):
    mu = jnp.mean(z, axis=-1, keepdims=True)
    zc = z - mu
    var = jnp.mean(zc * zc, axis=-1, keepdims=True)
    return zc * lax.rsqrt(var + LN_EPS) * g + b


LN_ROWS = 64


def _ln_rows(src_ref, g_ref, b_ref, emit):
    g = g_ref[...]
    b = b_ref[...]
    for r0 in range(0, src_ref.shape[0], LN_ROWS):
        rows = slice(r0, r0 + LN_ROWS)
        emit(rows, _layer_norm(src_ref[rows, :], g, b))


def _modulate(x, mod):
    shift = mod[:, 0:D_MODEL]
    scale = mod[:, D_MODEL:2 * D_MODEL]
    return x * (1.0 + scale) + shift


def _gate(mod):
    return 1.0 + mod[:, 2 * D_MODEL:3 * D_MODEL]


MLP_UP = 512
MLP_DOWN = 256


def _mlp_pieces(src_ref, g_ref, b_ref, mod, w1_ref, b1_ref, w2_ref, b2_ref, xn_ref, u_ref, hid_ref, z_ref):
    gate = _gate(mod)

    def prepare():
        def emit(rows, xn):
            xn_ref[rows, :] = xn
            u_ref[rows, :] = _modulate(xn, mod).astype(BF16)
        _ln_rows(src_ref, g_ref, b_ref, emit)

    def up(f):
        f0 = f * MLP_UP
        hdn = _bdot(u_ref[...], w1_ref[:, f0:f0 + MLP_UP]) + b1_ref[:, f0:f0 + MLP_UP]
        hdn = jnp.maximum(hdn, 0.0)
        hid_ref[:, f0:f0 + MLP_UP] = (hdn * hdn).astype(BF16)

    def down(n):
        n0 = n * MLP_DOWN
        y = _bdot(hid_ref[...], w2_ref[:, n0:n0 + MLP_DOWN]) + b2_ref[:, n0:n0 + MLP_DOWN]
        z_ref[:, n0:n0 + MLP_DOWN] = ALPHA * xn_ref[:, n0:n0 + MLP_DOWN] + gate[:, n0:n0 + MLP_DOWN] * y

    return prepare, up, down


ADA_TN = 1024


def _ada_kernel(c_ref, w_ref, b_ref, o_ref):
    c = c_ref[...]
    s = c * _sigmoid(c)
    o_ref[...] = _dot_split(s, w_ref[...]) + b_ref[...]


def _ada_call(c, ada_w, ada_b):
    n_mod = ada_w.shape[0]
    bsz = c.shape[0]
    return pl.pallas_call(
        _ada_kernel,
        grid=(n_mod, 3 * D_MODEL // ADA_TN),
        in_specs=[
            pl.BlockSpec((bsz, D_MODEL), lambda i, j: (0, 0)),
            pl.BlockSpec((None, D_MODEL, ADA_TN), lambda i, j: (i, 0, j)),
            pl.BlockSpec((None, 1, ADA_TN), lambda i, j: (i, 0, j)),
        ],
        out_specs=pl.BlockSpec((None, bsz, ADA_TN), lambda i, j: (i, 0, j)),
        out_shape=jax.ShapeDtypeStruct((n_mod, bsz, 3 * D_MODEL), F32),
        compiler_params=_cparams(("arbitrary", "arbitrary")),
        name="ada",
    )(c, ada_w, ada_b)


EV_T = 512
EV_CHUNKS = EV_T // GLA_CHUNK
_XR, _YR = 0, RG_WIDTH
_Q = 2 * RG_WIDTH
_K = _Q + GLA_KDIM
_V = _K + GLA_KDIM
_G = _V + GLA_VDIM
_ZL = _G + GLA_VDIM


def _scan_rows8(a, u):
    rows, ch = a.shape
    a3 = a.reshape(rows // SUBLANES, SUBLANES, ch)
    u3 = u.reshape(rows // SUBLANES, SUBLANES, ch)
    sub = lax.broadcasted_iota(jnp.int32, a3.shape, 1)
    d = 1
    while d < SUBLANES:
        keep = sub >= d
        u3 = jnp.where(keep, a3 * pltpu.roll(u3, d, 1) + u3, u3)
        a3 = jnp.where(keep, a3 * pltpu.roll(a3, d, 1), a3)
        d *= 2
    return a3, u3


def _even_layer_kernel(x_ref, mod_ref, win_ref, convw_ref, convb_ref, wg_ref, ba_ref, bx_ref, lam_ref,
                       wup_ref, bup_ref, gn_ref, wout_ref, lng_ref, lnb_ref,
                       mod2_ref, w1_ref, b1_ref, w2_ref, b2_ref, lng2_ref, lnb2_ref, o_ref,
                       xmid_ref, xn_ref, umlp_ref, hid_ref, zmlp_ref, p_ref, mix_ref, tail_ref, hc_ref, st_ref,
                       qd_ref, kd_ref, ke_ref, v_ref, bc_ref, oo_ref, *, tiles_per_seq, n_tiles):
    step = pl.program_id(0)
    t_idx = jnp.minimum(step, n_tiles - 1) % tiles_per_seq

    @pl.when(step == 0)
    def _():
        xmid_ref[...] = jnp.zeros_like(xmid_ref)

    @pl.when(t_idx == 0)
    def _():
        tail_ref[...] = jnp.zeros_like(tail_ref)
        hc_ref[...] = jnp.zeros_like(hc_ref)
        st_ref[...] = jnp.zeros_like(st_ref)

    mlp_prepare, mlp_up, mlp_down = _mlp_pieces(xmid_ref, lng_ref, lnb_ref, mod2_ref[...], w1_ref, b1_ref,
                                                w2_ref, b2_ref, xn_ref, umlp_ref, hid_ref, zmlp_ref)
    mlp_prepare()

    u_in = _modulate(x_ref[...], mod_ref[...]).astype(BF16)
    for c0 in range(0, EVEN_IN_PAD, PROJ_TN):
        c1 = min(c0 + PROJ_TN, EVEN_IN_PAD)
        p_ref[:, c0:c1] = _bdot(u_in, win_ref[:, c0:c1])

    xr = p_ref[:, _XR:_XR + RG_WIDTH]
    tail = tail_ref[...]
    row8 = lax.broadcasted_iota(jnp.int32, (SUBLANES, RG_WIDTH), 0)
    xc = convb_ref[...] + convw_ref[CONV_WIDTH - 1:CONV_WIDTH, :] * xr
    for j in range(1, CONV_WIDTH):
        xs = pltpu.roll(xr, j, 0)
        head = jnp.where(row8 < j, pltpu.roll(tail, j, 0), xs[0:SUBLANES])
        xs = jnp.concatenate([head, xs[SUBLANES:]], axis=0)
        xc = xc + convw_ref[CONV_WIDTH - 1 - j:CONV_WIDTH - j, :] * xs
    tail_ref[...] = xr[EV_T - SUBLANES:EV_T]

    half = RG_WIDTH // 2
    r_parts, i_parts = [], []
    for j in range(2):
        ri = _bdot(xc[:, j * half:(j + 1) * half].astype(BF16), wg_ref[j])
        r_parts.append(ri[:, :half])
        i_parts.append(ri[:, half:])
    for f in range(3):
        mlp_up(f)
    r = _sigmoid(jnp.concatenate(r_parts, axis=1) + ba_ref[...])
    ig = _sigmoid(jnp.concatenate(i_parts, axis=1) + bx_ref[...])
    log_a = RG_C * r * _log_sigmoid(lam_ref[...])
    a = jnp.exp(log_a)
    u = _sqrt_nonneg(-jnp.tanh(log_a) * (a * a + 1.0)) * (ig * xc)

    a3, h3 = _scan_rows8(a, u)
    carry = jnp.broadcast_to(hc_ref[SUBLANES - 1:SUBLANES, :], (SUBLANES, RG_WIDTH))
    h_groups = []
    n_groups = EV_T // SUBLANES
    for g in range(n_groups):
        hg = h3[g] + a3[g] * carry
        h_groups.append(hg)
        carry = jnp.broadcast_to(hg[SUBLANES - 1:SUBLANES, :], (SUBLANES, RG_WIDTH))
        if g == n_groups // 3:
            mlp_up(3)
        if g == 2 * n_groups // 3:
            mlp_up(4)
    hc_ref[...] = carry
    h = jnp.concatenate(h_groups, axis=0)
    mix_ref[:, 0:RG_WIDTH] = (h * _gelu_tanh(p_ref[:, _YR:_YR + RG_WIDTH])).astype(BF16)
    mlp_up(5)

    z = _dot_split(p_ref[:, _ZL:_ZL + LANES], wup_ref[...]) + bup_ref[...]
    bc = _log_sigmoid(z) * (1.0 / GLA_GATE_TAU)
    rowk = lax.broadcasted_iota(jnp.int32, (EV_T, GLA_KDIM), 0) & (GLA_CHUNK - 1)
    d = 1
    while d < GLA_CHUNK:
        bc = jnp.where(rowk >= d, bc + pltpu.roll(bc, d, 0), bc)
        d *= 2
    bc3 = bc.reshape(EV_CHUNKS, GLA_CHUNK, GLA_KDIM)
    b_last = jnp.broadcast_to(bc3[:, GLA_CHUNK - 1:GLA_CHUNK, :], bc3.shape).reshape(EV_T, GLA_KDIM)
    q = p_ref[:, _Q:_Q + GLA_KDIM]
    k = p_ref[:, _K:_K + GLA_KDIM]
    qd_ref[...] = (q * (GLA_HK ** -0.5) * jnp.exp(bc)).astype(BF16)
    kd_ref[...] = (k * jnp.exp(-bc)).astype(BF16)
    ke_ref[...] = (k * jnp.exp(b_last - bc)).astype(BF16)
    v_ref[...] = p_ref[:, _V:_V + GLA_VDIM].astype(BF16)
    bc_ref[...] = bc

    lane_k = lax.broadcasted_iota(jnp.int32, (GLA_CHUNK, GLA_KDIM), 1) // GLA_HK
    lane_v = lax.broadcasted_iota(jnp.int32, (GLA_CHUNK, GLA_VDIM), 1) // GLA_HV
    tri_r = lax.broadcasted_iota(jnp.int32, (GLA_CHUNK, GLA_HEADS * GLA_CHUNK), 0)
    tri_c = lax.broadcasted_iota(jnp.int32, (GLA_CHUNK, GLA_HEADS * GLA_CHUNK), 1) & (GLA_CHUNK - 1)
    st_r = lax.broadcasted_iota(jnp.int32, (GLA_VDIM, GLA_KDIM), 0) // GLA_HV
    st_c = lax.broadcasted_iota(jnp.int32, (GLA_VDIM, GLA_KDIM), 1) // GLA_HK
    nt = (((1,), (1,)), ((), ()))
    tn = (((0,), (0,)), ((), ()))

    def chunk_body(c):
        r0 = c * GLA_CHUNK
        qd = qd_ref[r0:r0 + GLA_CHUNK, :]
        kd = kd_ref[r0:r0 + GLA_CHUNK, :]
        ke = ke_ref[r0:r0 + GLA_CHUNK, :]
        vv = v_ref[r0:r0 + GLA_CHUNK, :]
        zk = jnp.zeros_like(kd)
        zv = jnp.zeros_like(vv)
        kbd = jnp.concatenate([jnp.where(lane_k == hh, kd, zk) for hh in range(GLA_HEADS)], axis=0)
        att = lax.dot_general(qd, kbd, nt, preferred_element_type=F32)
        att = jnp.where(tri_c <= tri_r, att, 0.0).astype(BF16)
        vbd = jnp.concatenate([jnp.where(lane_v == hh, vv, zv) for hh in range(GLA_HEADS)], axis=0)
        st = st_ref[...]
        o = _bdot(att, vbd) + lax.dot_general(qd, st.astype(BF16), nt, preferred_element_type=F32)
        oo_ref[r0:r0 + GLA_CHUNK, :] = o
        kv_t = lax.dot_general(vv, ke, tn, preferred_element_type=F32)
        decay = jnp.exp(bc_ref[r0 + GLA_CHUNK - 1:r0 + GLA_CHUNK, :])
        st_ref[...] = decay * st + jnp.where(st_r == st_c, kv_t, 0.0)

    mlp_up(6)
    mlp_up(7)
    for c in range(EV_CHUNKS):
        chunk_body(c)
        if c % 2 == 1:
            mlp_down(c // 2)

    for hh in range(GLA_HEADS):
        oh = oo_ref[:, hh * GLA_HV:(hh + 1) * GLA_HV]
        ms = jnp.mean(oh * oh, axis=-1, keepdims=True)
        gg = p_ref[:, _G + hh * GLA_HV:_G + (hh + 1) * GLA_HV]
        on = oh * lax.rsqrt(ms + RMS_EPS) * gn_ref[:, hh * GLA_HV:(hh + 1) * GLA_HV]
        mix_ref[:, RG_WIDTH + hh * GLA_HV:RG_WIDTH + (hh + 1) * GLA_HV] = (on * (gg * _sigmoid(gg))).astype(BF16)

    y = _bdot(mix_ref[...], wout_ref[...])

    def emit_out(rows, xn):
        o_ref[rows, :] = xn
    _ln_rows(zmlp_ref, lng2_ref, lnb2_ref, emit_out)
    xmid_ref[...] = ALPHA * x_ref[...] + _gate(mod_ref[...]) * y


def _even_layer_call(x, mod, mixer_params, mod_mlp, mlp_params):
    bsz, seq, _ = x.shape
    tiles_per_seq = seq // EV_T
    n_tiles = bsz * tiles_per_seq

    def mix_tile(s):
        return jnp.minimum(s, n_tiles - 1)

    def mlp_tile(s):
        return jnp.maximum(s - 1, 0)

    def tile_spec(tile_of):
        return pl.BlockSpec((None, EV_T, D_MODEL),
                            lambda s: (tile_of(s) // tiles_per_seq, tile_of(s) % tiles_per_seq, 0))

    def mod_spec(tile_of):
        return pl.BlockSpec((None, 1, 3 * D_MODEL), lambda s: (tile_of(s) // tiles_per_seq, 0, 0))

    return pl.pallas_call(
        functools.partial(_even_layer_kernel, tiles_per_seq=tiles_per_seq, n_tiles=n_tiles),
        grid=(n_tiles + 1,),
        in_specs=[tile_spec(mix_tile), mod_spec(mix_tile)]
        + [_resident(a.shape) for a in mixer_params] + [mod_spec(mlp_tile)]
        + [_resident(a.shape) for a in mlp_params],
        out_specs=tile_spec(mlp_tile),
        out_shape=jax.ShapeDtypeStruct((bsz, seq, D_MODEL), F32),
        scratch_shapes=[
            pltpu.VMEM((EV_T, D_MODEL), F32),
            pltpu.VMEM((EV_T, D_MODEL), F32),
            pltpu.VMEM((EV_T, D_MODEL), BF16),
            pltpu.VMEM((EV_T, D_FF), BF16),
            pltpu.VMEM((EV_T, D_MODEL), F32),
            pltpu.VMEM((EV_T, EVEN_IN_PAD), F32),
            pltpu.VMEM((EV_T, D_MODEL), BF16),
            pltpu.VMEM((SUBLANES, RG_WIDTH), F32),
            pltpu.VMEM((SUBLANES, RG_WIDTH), F32),
            pltpu.VMEM((GLA_VDIM, GLA_KDIM), F32),
            pltpu.VMEM((EV_T, GLA_KDIM), BF16),
            pltpu.VMEM((EV_T, GLA_KDIM), BF16),
            pltpu.VMEM((EV_T, GLA_KDIM), BF16),
            pltpu.VMEM((EV_T, GLA_VDIM), BF16),
            pltpu.VMEM((EV_T, GLA_KDIM), F32),
            pltpu.VMEM((EV_T, GLA_VDIM), F32),
        ],
        compiler_params=_cparams(("arbitrary",)),
        name="even_layer",
    )(x, mod, *mixer_params, mod_mlp, *mlp_params)


POST_TM = 512


def _post_kernel(a_ref, wout_ref, x_ref, mod_ref, lng_ref, lnb_ref,
                 mod2_ref, w1_ref, b1_ref, w2_ref, b2_ref, lng2_ref, lnb2_ref, o_ref,
                 zmix_ref, xn_ref, umlp_ref, hid_ref, zmlp_ref):
    y = _bdot(a_ref[...], wout_ref[...])
    zmix_ref[...] = ALPHA * x_ref[...] + _gate(mod_ref[...]) * y
    mlp_prepare, mlp_up, mlp_down = _mlp_pieces(zmix_ref, lng_ref, lnb_ref, mod2_ref[...], w1_ref, b1_ref,
                                                w2_ref, b2_ref, xn_ref, umlp_ref, hid_ref, zmlp_ref)
    mlp_prepare()
    for f in range(D_FF // MLP_UP):
        mlp_up(f)
    for n in range(D_MODEL // MLP_DOWN):
        mlp_down(n)

    def emit_out(rows, xn):
        o_ref[rows, :] = xn
    _ln_rows(zmlp_ref, lng2_ref, lnb2_ref, emit_out)


def _post_call(a, w_out, x, mod, ln_g, ln_b, mod_mlp, mlp_params):
    bsz, seq, _ = x.shape
    tile = pl.BlockSpec((None, POST_TM, D_MODEL), lambda b, t: (b, t, 0))
    mod_spec = pl.BlockSpec((None, 1, 3 * D_MODEL), lambda b, t: (b, 0, 0))
    return pl.pallas_call(
        _post_kernel,
        grid=(bsz, seq // POST_TM),
        in_specs=[tile, _resident(w_out.shape), tile, mod_spec, _resident(ln_g.shape), _resident(ln_b.shape),
                  mod_spec] + [_resident(p.shape) for p in mlp_params],
        out_specs=tile,
        out_shape=jax.ShapeDtypeStruct((bsz, seq, D_MODEL), F32),
        scratch_shapes=[
            pltpu.VMEM((POST_TM, D_MODEL), F32),
            pltpu.VMEM((POST_TM, D_MODEL), F32),
            pltpu.VMEM((POST_TM, D_MODEL), BF16),
            pltpu.VMEM((POST_TM, D_FF), BF16),
            pltpu.VMEM((POST_TM, D_MODEL), F32),
        ],
        compiler_params=_cparams(("arbitrary", "arbitrary")),
        name="post",
    )(a, w_out, x, mod, ln_g, ln_b, mod_mlp, *mlp_params)


FP_T = 512
FP_GW = 256
FP_GH = FP_GW // FOX_HD
_OQ, _OK, _OV, _OG, _OF = 0, D_MODEL, 2 * D_MODEL, 3 * D_MODEL, 4 * D_MODEL
F_ONE_COL = F_PARTS * FOX_HEADS


def _fox_place_matrix():
    place = np.zeros((LANES, FOX_HEADS * LANES), np.float32)
    for h in range(FOX_HEADS):
        base = h * LANES + (FOX_HD if h % 2 == 0 else 0)
        for part in range(F_PARTS):
            place[part * FOX_HEADS + h, base + part] = 1.0
            place[F_ONE_COL, base + F_PARTS + part] = 1.0
            place[F_ONE_COL, base + F_SLOTS + part] = 1.0
            place[part * FOX_HEADS + h, base + F_SLOTS + F_PARTS + part] = -1.0
    return place


def _fox_prep_kernel(x_ref, mod_ref, win_ref, bf_ref, gq_ref, gk_ref, place_ref,
                     qa_ref, ka_ref, vt_ref, og_ref, fc_ref):
    t_idx = pl.program_id(1)

    @pl.when(t_idx == 0)
    def _():
        fc_ref[...] = jnp.zeros_like(fc_ref)

    u_in = _modulate(x_ref[...], mod_ref[...]).astype(BF16)

    def proj(c0, width):
        return _bdot(u_in, win_ref[:, c0:c0 + width])

    lane = lax.broadcasted_iota(jnp.int32, (FP_T, LANES), 1)
    f = _log_sigmoid(proj(_OF, LANES) + bf_ref[...])
    row = lax.broadcasted_iota(jnp.int32, (FP_T, LANES), 0)
    d = 1
    while d < FP_T:
        f = jnp.where(row >= d, f + pltpu.roll(f, d, 0), f)
        d *= 2
    f = f + fc_ref[SUBLANES - 1:SUBLANES, :]
    fc_ref[...] = f[FP_T - SUBLANES:FP_T]
    f = jnp.where(lane < FOX_HEADS, f * LOG2E, 0.0)
    f1 = f.astype(BF16).astype(F32)
    r1 = f - f1
    f2 = r1.astype(BF16).astype(F32)
    f3 = (r1 - f2).astype(BF16).astype(F32)
    packed = f1 + pltpu.roll(f2, FOX_HEADS, 1) + pltpu.roll(f3, 2 * FOX_HEADS, 1)
    packed = jnp.where(lane == F_ONE_COL, 1.0, packed).astype(BF16)

    left = lane < FOX_HD

    def head_norm(xp, g):
        sq = xp * xp
        ms_l = jnp.sum(jnp.where(left, sq, 0.0), axis=-1, keepdims=True)
        ms_r = jnp.sum(jnp.where(left, 0.0, sq), axis=-1, keepdims=True)
        ms = jnp.where(left, ms_l, ms_r) * (1.0 / FOX_HD)
        return xp * lax.rsqrt(ms + RMS_EPS) * g

    def group_products(gi):
        c0 = gi * FP_GW
        h0 = c0 // FOX_HD
        return (proj(_OQ + c0, FP_GW), proj(_OK + c0, FP_GW), proj(_OV + c0, FP_GW), proj(_OG + c0, FP_GW),
                _bdot(packed, place_ref[:, h0 * LANES:(h0 + FP_GH) * LANES]))

    def group_finish(gi, prods):
        qg, kg, vg, gg, feat = prods
        c0 = gi * FP_GW
        og_ref[:, c0:c0 + FP_GW] = _sigmoid(gg).astype(BF16)
        for pi in range(FP_GW // LANES):
            l0 = pi * LANES
            qn = head_norm(qg[:, l0:l0 + LANES], gq_ref[...]) * (FOX_HD ** -0.5 * LOG2E)
            kn = head_norm(kg[:, l0:l0 + LANES], gk_ref[...])
            vv = vg[:, l0:l0 + LANES]
            for e in range(2):
                hl = 2 * pi + e
                h = gi * FP_GH + hl
                own = left if e == 0 else jnp.logical_not(left)
                fh = feat[:, hl * LANES:(hl + 1) * LANES]
                qa_ref[h] = jnp.where(own, qn, fh).astype(BF16)
                ka_ref[h] = jnp.where(own, kn, pltpu.roll(fh, LANES - F_SLOTS, 1)).astype(BF16)
                vt_ref[h, 0] = jnp.where(own, vv, 1.0).T.astype(BF16)

    n_groups = D_MODEL // FP_GW
    pending = group_products(0)
    for gi in range(n_groups):
        nxt = group_products(gi + 1) if gi + 1 < n_groups else None
        group_finish(gi, pending)
        pending = nxt


def _fox_prep_call(x, mod, w_in, b_f, gq, gk, place):
    bsz, seq, _ = x.shape
    small = [w_in, b_f, gq, gk, place]
    return pl.pallas_call(
        _fox_prep_kernel,
        grid=(bsz, seq // FP_T),
        in_specs=[pl.BlockSpec((None, FP_T, D_MODEL), lambda b, t: (b, t, 0)),
                  pl.BlockSpec((None, 1, 3 * D_MODEL), lambda b, t: (b, 0, 0))]
        + [_resident(a.shape) for a in small],
        out_specs=[
            pl.BlockSpec((None, FOX_HEADS, FP_T, LANES), lambda b, t: (b, 0, t, 0)),
            pl.BlockSpec((None, FOX_HEADS, FP_T, LANES), lambda b, t: (b, 0, t, 0)),
            pl.BlockSpec((None, FOX_HEADS, 1, LANES, FP_T), lambda b, t: (b, 0, t, 0, 0)),
            pl.BlockSpec((None, FP_T, D_MODEL), lambda b, t: (b, t, 0)),
        ],
        out_shape=[
            jax.ShapeDtypeStruct((bsz, FOX_HEADS, seq, LANES), BF16),
            jax.ShapeDtypeStruct((bsz, FOX_HEADS, seq, LANES), BF16),
            jax.ShapeDtypeStruct((bsz, FOX_HEADS, seq // FP_T, LANES, FP_T), BF16),
            jax.ShapeDtypeStruct((bsz, seq, D_MODEL), BF16),
        ],
        scratch_shapes=[pltpu.VMEM((SUBLANES, LANES), F32)],
        compiler_params=_cparams(("arbitrary", "arbitrary")),
        name="fox_prep",
    )(x, mod, *small)


FA_T = FP_T
FA_NH = 4
FA_NP = FA_NH // 2
FA_GROUP = 3


def _fox_attn_kernel(qa_ref, ka_ref, vt_ref, g_ref, o_ref, m_ref, acc_ref, *s_refs):
    seq = qa_ref.shape[1]
    n_q = seq // FA_T
    n_items = n_q * (n_q + 1) // 2
    nt = (((1,), (1,)), ((), ()))
    key = lax.broadcasted_iota(jnp.int32, (FA_T, FA_T), 0)
    qry = lax.broadcasted_iota(jnp.int32, (FA_T, FA_T), 1)
    top = lax.broadcasted_iota(jnp.int32, (LANES, FA_T), 0) < FOX_HD

    def reset_stats():
        m_ref[...] = jnp.full_like(m_ref, MASK_NEG)
        acc_ref[...] = jnp.zeros_like(acc_ref)

    def scores(qi, j, dst_ref, h):
        q0 = pl.multiple_of(qi * FA_T, FA_T)
        k0 = pl.multiple_of(j * FA_T, FA_T)
        dst_ref[h] = lax.dot_general(ka_ref[h, pl.ds(k0, FA_T), :], qa_ref[h, pl.ds(q0, FA_T), :], nt,
                                     preferred_element_type=F32)

    def softmax_pv(j, src_ref, h, masked):
        s_t = src_ref[h]
        if masked:
            s_t = jnp.where(key <= qry, s_t, MASK_NEG)
        m_prev = m_ref[h]
        m_new = jnp.maximum(m_prev, jnp.max(s_t, axis=0, keepdims=True))
        p_t = jnp.exp2(s_t - m_new).astype(BF16)
        acc_ref[h] = jnp.exp2(m_prev - m_new) * acc_ref[h] + _bdot(vt_ref[h, j], p_t)
        m_ref[h] = m_new

    def finish_tile(qi):
        q0 = pl.multiple_of(qi * FA_T, FA_T)
        for p in range(FA_NP):
            acc0 = acc_ref[2 * p]
            acc1 = acc_ref[2 * p + 1]
            o_t = jnp.where(top, acc0 * (1.0 / acc0[FOX_HD:FOX_HD + 1, :]), acc1 * (1.0 / acc1[0:1, :]))
            g = g_ref[pl.ds(q0, FA_T), p * LANES:(p + 1) * LANES].astype(F32)
            o_ref[pl.ds(q0, FA_T), p * LANES:(p + 1) * LANES] = (o_t.T * g).astype(BF16)
        reset_stats()

    def advance(qi, j):
        diag = j == qi
        return diag, jnp.where(diag, jnp.minimum(qi + 1, n_q - 1), qi), jnp.where(diag, 0, j + 1)

    def item(qi, j, cur_ref, masked, nxt, nxt_ref):
        for h in range(FA_NH):
            scores(nxt[0], nxt[1], nxt_ref, h)
            softmax_pv(j, cur_ref, h, masked)
        if masked:
            finish_tile(qi)

    reset_stats()
    for h in range(FA_NH):
        scores(0, 0, s_refs[0], h)

    patterns = [p for p in itertools.product((False, True), repeat=FA_GROUP)
                if not any(p[k] and p[k + 1] for k in range(FA_GROUP - 1))]

    def group_body(i, carry):
        qs, js, diags = [carry[0]], [carry[1]], []
        for _ in range(FA_GROUP):
            diag, q_n, j_n = advance(qs[-1], js[-1])
            diags.append(diag)
            qs.append(q_n)
            js.append(j_n)

        def run(pattern):
            for k in range(FA_GROUP):
                item(qs[k], js[k], s_refs[k % len(s_refs)], pattern[k], (qs[k + 1], js[k + 1]),
                     s_refs[(k + 1) % len(s_refs)])

        for pattern in patterns:
            cond = functools.reduce(jnp.logical_and,
                                    [d if m else jnp.logical_not(d) for d, m in zip(diags, pattern)])
            pl.when(cond)(functools.partial(run, pattern))
        return qs[-1], js[-1]

    assert n_q >= 2 and n_items % FA_GROUP == 0 and FA_GROUP % len(s_refs) == 0
    lax.fori_loop(0, n_items // FA_GROUP, group_body, (jnp.int32(0), jnp.int32(0)))


def _fox_attn_call(qa, ka, vt, og):
    bsz, _, seq, _ = qa.shape
    gw = FA_NP * LANES
    return pl.pallas_call(
        _fox_attn_kernel,
        grid=(bsz, FOX_HEADS // FA_NH),
        in_specs=[
            pl.BlockSpec((None, FA_NH, seq, LANES), lambda b, p: (b, p, 0, 0)),
            pl.BlockSpec((None, FA_NH, seq, LANES), lambda b, p: (b, p, 0, 0)),
            pl.BlockSpec((None, FA_NH, seq // FA_T, LANES, FA_T), lambda b, p: (b, p, 0, 0, 0)),
            pl.BlockSpec((None, seq, gw), lambda b, p: (b, 0, p)),
        ],
        out_specs=pl.BlockSpec((None, seq, gw), lambda b, p: (b, 0, p)),
        out_shape=jax.ShapeDtypeStruct((bsz, seq, D_MODEL), BF16),
        scratch_shapes=[
            pltpu.VMEM((FA_NH, 1, FA_T), F32),
            pltpu.VMEM((FA_NH, LANES, FA_T), F32),
        ] + [pltpu.VMEM((FA_NH, FA_T, FA_T), F32)] * FA_GROUP,
        compiler_params=_cparams(("arbitrary", "arbitrary")),
        name="fox_attn",
    )(qa, ka, vt, og)


def _pad_cols(w, n):
    return jnp.pad(w, ((0, 0), (0, n - w.shape[1])))


def _gate_weights(w_a, w_x):
    per_half = RG_BLOCKS // 2
    half = RG_WIDTH // 2

    def bd(w, j):
        m = jnp.zeros((half, half), F32)
        for i in range(per_half):
            m = lax.dynamic_update_slice(m, w[j * per_half + i], (i * RG_BLOCK, i * RG_BLOCK))
        return m

    return jnp.stack([jnp.concatenate([bd(w_a, j), bd(w_x, j)], axis=1) for j in range(2)]).astype(BF16)


def kernel(x, c, ada_w, ada_b, ln_g, ln_b, ev_w_in, ev_conv_w, ev_conv_b, ev_rg_wa, ev_rg_ba, ev_rg_wx,
           ev_rg_bx, ev_rg_lam, ev_gla_w_up, ev_gla_b_up, ev_gla_norm_g, ev_w_out, od_w_in, od_b_f,
           od_q_norm_g, od_k_norm_g, od_w_out, mlp_w1, mlp_b1, mlp_w2, mlp_b2):
    bsz = x.shape[0]
    mods = _ada_call(c, ada_w.reshape(2 * DEPTH, D_MODEL, 3 * D_MODEL),
                     ada_b.reshape(2 * DEPTH, 1, 3 * D_MODEL))
    mods = mods.reshape(2 * DEPTH, bsz, 1, 3 * D_MODEL)
    place = jnp.asarray(_fox_place_matrix(), BF16)

    for layer in range(DEPTH):
        mod_mix = mods[2 * layer]
        mod_mlp = mods[2 * layer + 1]
        mlp_params = [mlp_w1[layer].astype(BF16), mlp_b1[layer][None], mlp_w2[layer].astype(BF16),
                      mlp_b2[layer][None], ln_g[layer, 1][None], ln_b[layer, 1][None]]
        if layer % 2 == 0:
            e = layer // 2
            w_up = jnp.pad(ev_gla_w_up[e], ((0, LANES - GLA_LOWRANK), (0, 0)))
            mixer_params = [
                _pad_cols(ev_w_in[e], EVEN_IN_PAD).astype(BF16), ev_conv_w[e], ev_conv_b[e][None],
                _gate_weights(ev_rg_wa[e], ev_rg_wx[e]), ev_rg_ba[e][None], ev_rg_bx[e][None],
                ev_rg_lam[e][None], w_up, ev_gla_b_up[e][None], ev_gla_norm_g[e][None],
                ev_w_out[e].astype(BF16), ln_g[layer, 0][None], ln_b[layer, 0][None]]
            x = _even_layer_call(x, mod_mix, mixer_params, mod_mlp, mlp_params)
        else:
            o = layer // 2
            w_in = _pad_cols(od_w_in[o], ODD_IN_PAD).astype(BF16)
            b_f = jnp.pad(od_b_f[o], (0, LANES - FOX_HEADS))[None]
            gq = jnp.tile(od_q_norm_g[o], 2)[None]
            gk = jnp.tile(od_k_norm_g[o], 2)[None]
            qa, ka, vt, og = _fox_prep_call(x, mod_mix, w_in, b_f, gq, gk, place)
            mix = _fox_attn_call(qa, ka, vt, og)
            x = _post_call(mix, od_w_out[o].astype(BF16), x, mod_mix, ln_g[layer, 0][None],
                           ln_b[layer, 0][None], mod_mlp, mlp_params)
    return x
```

```python
import functools
import itertools

import numpy as np
import jax
import jax.numpy as jnp
from jax import lax
from jax.experimental import pallas as pl
from jax.experimental.pallas import tpu as pltpu

F32 = jnp.float32
BF16 = jnp.bfloat16

D_MODEL = 1024
DEPTH = 2
RG_WIDTH = D_MODEL // 2
RG_BLOCKS = 8
RG_BLOCK = RG_WIDTH // RG_BLOCKS
CONV_WIDTH = 4
RG_C = 8.0
GLA_HEADS = 4
GLA_VDIM = D_MODEL // 2
GLA_KDIM = GLA_VDIM // 2
GLA_HK = GLA_KDIM // GLA_HEADS
GLA_HV = GLA_VDIM // GLA_HEADS
GLA_LOWRANK = 16
GLA_GATE_TAU = 16.0
GLA_CHUNK = 64
EVEN_IN = 2 * RG_WIDTH + 2 * GLA_KDIM + 2 * GLA_VDIM + GLA_LOWRANK
FOX_HEADS = 16
FOX_HD = D_MODEL // FOX_HEADS
FOX_PAIRS = FOX_HEADS // 2
ODD_IN = 4 * D_MODEL + FOX_HEADS
D_FF = 4 * D_MODEL
ALPHA = (2 * DEPTH) ** 0.25
LN_EPS = 1e-5
RMS_EPS = 1e-6

LANES = 128
SUBLANES = 8
VMEM_BYTES = 64 * 1024 * 1024
VMEM_CAP = VMEM_BYTES - 8 * 1024 * 1024
VMEM_TEMP = 16 * 1024 * 1024

EVEN_IN_PAD = EVEN_IN + (LANES - GLA_LOWRANK)
ODD_IN_PAD = ODD_IN + (LANES - FOX_HEADS)
MASK_NEG = -1e30
LOG2E = float(np.log2(np.e))

F_PARTS = 3
F_SLOTS = 8

PROJ_TN = 512


def _tiled_call(body, *, grid, in_specs, out_specs, out_shape, scratch_shapes=(), name):
    out_specs_l = list(out_specs) if isinstance(out_specs, (list, tuple)) else [out_specs]
    out_shape_l = list(out_shape) if isinstance(out_shape, (list, tuple)) else [out_shape]

    def run(*operands):
        need = VMEM_TEMP
        for spec, arr in zip(list(in_specs) + out_specs_l, list(operands) + out_shape_l):
            n_buf = 2 if spec.pipeline_mode is None else 1
            block = [d for d in spec.block_shape if d is not None]
            need += n_buf * int(np.prod(block)) * jnp.dtype(arr.dtype).itemsize
        for ref in scratch_shapes:
            need += int(np.prod(ref.shape)) * jnp.dtype(ref.dtype).itemsize
        params = pltpu.CompilerParams(dimension_semantics=("arbitrary",) * len(grid),
                                      vmem_limit_bytes=min(need, VMEM_CAP))
        return pl.pallas_call(body, grid=grid, in_specs=in_specs, out_specs=out_specs, out_shape=out_shape,
                              scratch_shapes=list(scratch_shapes), compiler_params=params, name=name)(*operands)

    return run


def _resident(shape):
    nd = len(shape)
    return pl.BlockSpec(shape, lambda *_: (0,) * nd, pipeline_mode=pl.Buffered(1))


def _sigmoid(x):
    return 0.5 * jnp.tanh(0.5 * x) + 0.5


def _log_sigmoid(x):
    return jnp.minimum(x, 0.0) - jnp.log1p(jnp.exp(-jnp.abs(x)))


def _gelu_tanh(x):
    c = np.sqrt(2.0 / np.pi).astype(np.float32)
    return x * (0.5 * (1.0 + jnp.tanh(c * (x + 0.044715 * (x * x * x)))))


def _sqrt_nonneg(y):
    return jnp.where(y > 0.0, y * lax.rsqrt(y), 0.0)


def _bdot(a, b):
    return jnp.dot(a, b, preferred_element_type=F32)


def _split2(a):
    hi = a.astype(BF16)
    lo = (a - hi.astype(F32)).astype(BF16)
    return hi, lo


def _dot_split(a, w):
    ah, al = _split2(a)
    wh, wl = _split2(w)
    return _bdot(ah, wh) + (_bdot(ah, wl) + _bdot(al, wh))


def _layer_norm(z, g, b):
    mu = jnp.mean(z, axis=-1, keepdims=True)
    zc = z - mu
    var = jnp.mean(zc * zc, axis=-1, keepdims=True)
    return zc * lax.rsqrt(var + LN_EPS) * g + b


LN_ROWS = 64


def _ln_rows(src_ref, g_ref, b_ref, emit):
    g = g_ref[...]
    b = b_ref[...]
    for r0 in range(0, src_ref.shape[0], LN_ROWS):
        rows = slice(r0, r0 + LN_ROWS)
        emit(rows, _layer_norm(src_ref[rows, :], g, b))


def _modulate(x, mod):
    shift = mod[:, 0:D_MODEL]
    scale = mod[:, D_MODEL:2 * D_MODEL]
    return x * (1.0 + scale) + shift


def _gate(mod):
    return 1.0 + mod[:, 2 * D_MODEL:3 * D_MODEL]


MLP_UP = 512
MLP_DOWN = 256


def _mlp_pieces(src_ref, g_ref, b_ref, mod, w1_ref, b1_ref, w2_ref, b2_ref, xn_ref, u_ref, hid_ref, z_ref):
    gate = _gate(mod)

    def prepare():
        def emit(rows, xn):
            xn_ref[rows, :] = xn
            u_ref[rows, :] = _modulate(xn, mod).astype(BF16)
        _ln_rows(src_ref, g_ref, b_ref, emit)

    def up(f):
        f0 = f * MLP_UP
        hdn = _bdot(u_ref[...], w1_ref[:, f0:f0 + MLP_UP]) + b1_ref[:, f0:f0 + MLP_UP]
        hdn = jnp.maximum(hdn, 0.0)
        hid_ref[:, f0:f0 + MLP_UP] = (hdn * hdn).astype(BF16)

    def down(n):
        n0 = n * MLP_DOWN
        y = _bdot(hid_ref[...], w2_ref[:, n0:n0 + MLP_DOWN]) + b2_ref[:, n0:n0 + MLP_DOWN]
        z_ref[:, n0:n0 + MLP_DOWN] = ALPHA * xn_ref[:, n0:n0 + MLP_DOWN] + gate[:, n0:n0 + MLP_DOWN] * y

    return prepare, up, down


ADA_TN = 1024


def _ada_kernel(c_ref, w_ref, b_ref, o_ref):
    c = c_ref[...]
    s = c * _sigmoid(c)
    o_ref[...] = _dot_split(s, w_ref[...]) + b_ref[...]


def _ada_call(c, ada_w, ada_b):
    n_mod = ada_w.shape[0]
    bsz = c.shape[0]
    return _tiled_call(
        _ada_kernel,
        grid=(n_mod, 3 * D_MODEL // ADA_TN),
        in_specs=[
            pl.BlockSpec((bsz, D_MODEL), lambda i, j: (0, 0)),
            pl.BlockSpec((None, D_MODEL, ADA_TN), lambda i, j: (i, 0, j)),
            pl.BlockSpec((None, 1, ADA_TN), lambda i, j: (i, 0, j)),
        ],
        out_specs=pl.BlockSpec((None, bsz, ADA_TN), lambda i, j: (i, 0, j)),
        out_shape=jax.ShapeDtypeStruct((n_mod, bsz, 3 * D_MODEL), F32),
        name="ada",
    )(c, ada_w, ada_b)


EV_T = 512
EV_CHUNKS = EV_T // GLA_CHUNK
_XR, _YR = 0, RG_WIDTH
_Q = 2 * RG_WIDTH
_K = _Q + GLA_KDIM
_V = _K + GLA_KDIM
_G = _V + GLA_VDIM
_ZL = _G + GLA_VDIM


def _scan_rows8(a, u):
    rows, ch = a.shape
    a3 = a.reshape(rows // SUBLANES, SUBLANES, ch)
    u3 = u.reshape(rows // SUBLANES, SUBLANES, ch)
    sub = lax.broadcasted_iota(jnp.int32, a3.shape, 1)
    d = 1
    while d < SUBLANES:
        keep = sub >= d
        u3 = jnp.where(keep, a3 * pltpu.roll(u3, d, 1) + u3, u3)
        a3 = jnp.where(keep, a3 * pltpu.roll(a3, d, 1), a3)
        d *= 2
    return a3, u3


def _even_layer_kernel(x_ref, mod_ref, win_ref, convw_ref, convb_ref, wg_ref, ba_ref, bx_ref, lam_ref,
                       wup_ref, bup_ref, gn_ref, wout_ref, lng_ref, lnb_ref,
                       mod2_ref, w1_ref, b1_ref, w2_ref, b2_ref, lng2_ref, lnb2_ref, o_ref,
                       xmid_ref, xn_ref, umlp_ref, hid_ref, zmlp_ref, p_ref, mix_ref, tail_ref, hc_ref, st_ref,
                       qd_ref, kd_ref, ke_ref, v_ref, bc_ref, oo_ref, *, tiles_per_seq, n_tiles):
    step = pl.program_id(0)
    t_idx = jnp.minimum(step, n_tiles - 1) % tiles_per_seq

    @pl.when(step == 0)
    def _():
        xmid_ref[...] = jnp.zeros_like(xmid_ref)

    @pl.when(t_idx == 0)
    def _():
        tail_ref[...] = jnp.zeros_like(tail_ref)
        hc_ref[...] = jnp.zeros_like(hc_ref)
        st_ref[...] = jnp.zeros_like(st_ref)

    mlp_prepare, mlp_up, mlp_down = _mlp_pieces(xmid_ref, lng_ref, lnb_ref, mod2_ref[...], w1_ref, b1_ref,
                                                w2_ref, b2_ref, xn_ref, umlp_ref, hid_ref, zmlp_ref)
    mlp_prepare()

    u_in = _modulate(x_ref[...], mod_ref[...]).astype(BF16)
    for c0 in range(0, EVEN_IN_PAD, PROJ_TN):
        c1 = min(c0 + PROJ_TN, EVEN_IN_PAD)
        p_ref[:, c0:c1] = _bdot(u_in, win_ref[:, c0:c1])

    xr = p_ref[:, _XR:_XR + RG_WIDTH]
    tail = tail_ref[...]
    row8 = lax.broadcasted_iota(jnp.int32, (SUBLANES, RG_WIDTH), 0)
    xc = convb_ref[...] + convw_ref[CONV_WIDTH - 1:CONV_WIDTH, :] * xr
    for j in range(1, CONV_WIDTH):
        xs = pltpu.roll(xr, j, 0)
        head = jnp.where(row8 < j, pltpu.roll(tail, j, 0), xs[0:SUBLANES])
        xs = jnp.concatenate([head, xs[SUBLANES:]], axis=0)
        xc = xc + convw_ref[CONV_WIDTH - 1 - j:CONV_WIDTH - j, :] * xs
    tail_ref[...] = xr[EV_T - SUBLANES:EV_T]

    half = RG_WIDTH // 2
    r_parts, i_parts = [], []
    for j in range(2):
        ri = _bdot(xc[:, j * half:(j + 1) * half].astype(BF16), wg_ref[j])
        r_parts.append(ri[:, :half])
        i_parts.append(ri[:, half:])
    for f in range(3):
        mlp_up(f)
    r = _sigmoid(jnp.concatenate(r_parts, axis=1) + ba_ref[...])
    ig = _sigmoid(jnp.concatenate(i_parts, axis=1) + bx_ref[...])
    log_a = RG_C * r * _log_sigmoid(lam_ref[...])
    a = jnp.exp(log_a)
    u = _sqrt_nonneg(-jnp.tanh(log_a) * (a * a + 1.0)) * (ig * xc)

    a3, h3 = _scan_rows8(a, u)
    carry = jnp.broadcast_to(hc_ref[SUBLANES - 1:SUBLANES, :], (SUBLANES, RG_WIDTH))
    h_groups = []
    n_groups = EV_T // SUBLANES
    for g in range(n_groups):
        hg = h3[g] + a3[g] * carry
        h_groups.append(hg)
        carry = jnp.broadcast_to(hg[SUBLANES - 1:SUBLANES, :], (SUBLANES, RG_WIDTH))
        if g == n_groups // 3:
            mlp_up(3)
        if g == 2 * n_groups // 3:
            mlp_up(4)
    hc_ref[...] = carry
    h = jnp.concatenate(h_groups, axis=0)
    mix_ref[:, 0:RG_WIDTH] = (h * _gelu_tanh(p_ref[:, _YR:_YR + RG_WIDTH])).astype(BF16)
    mlp_up(5)

    z = _dot_split(p_ref[:, _ZL:_ZL + LANES], wup_ref[...]) + bup_ref[...]
    bc = _log_sigmoid(z) * (1.0 / GLA_GATE_TAU)
    rowk = lax.broadcasted_iota(jnp.int32, (EV_T, GLA_KDIM), 0) & (GLA_CHUNK - 1)
    d = 1
    while d < GLA_CHUNK:
        bc = jnp.where(rowk >= d, bc + pltpu.roll(bc, d, 0), bc)
        d *= 2
    bc3 = bc.reshape(EV_CHUNKS, GLA_CHUNK, GLA_KDIM)
    b_last = jnp.broadcast_to(bc3[:, GLA_CHUNK - 1:GLA_CHUNK, :], bc3.shape).reshape(EV_T, GLA_KDIM)
    q = p_ref[:, _Q:_Q + GLA_KDIM]
    k = p_ref[:, _K:_K + GLA_KDIM]
    qd_ref[...] = (q * (GLA_HK ** -0.5) * jnp.exp(bc)).astype(BF16)
    kd_ref[...] = (k * jnp.exp(-bc)).astype(BF16)
    ke_ref[...] = (k * jnp.exp(b_last - bc)).astype(BF16)
    v_ref[...] = p_ref[:, _V:_V + GLA_VDIM].astype(BF16)
    bc_ref[...] = bc

    lane_k = lax.broadcasted_iota(jnp.int32, (GLA_CHUNK, GLA_KDIM), 1) // GLA_HK
    lane_v = lax.broadcasted_iota(jnp.int32, (GLA_CHUNK, GLA_VDIM), 1) // GLA_HV
    tri_r = lax.broadcasted_iota(jnp.int32, (GLA_CHUNK, GLA_HEADS * GLA_CHUNK), 0)
    tri_c = lax.broadcasted_iota(jnp.int32, (GLA_CHUNK, GLA_HEADS * GLA_CHUNK), 1) & (GLA_CHUNK - 1)
    st_r = lax.broadcasted_iota(jnp.int32, (GLA_VDIM, GLA_KDIM), 0) // GLA_HV
    st_c = lax.broadcasted_iota(jnp.int32, (GLA_VDIM, GLA_KDIM), 1) // GLA_HK
    nt = (((1,), (1,)), ((), ()))
    tn = (((0,), (0,)), ((), ()))

    def chunk_body(c):
        r0 = c * GLA_CHUNK
        qd = qd_ref[r0:r0 + GLA_CHUNK, :]
        kd = kd_ref[r0:r0 + GLA_CHUNK, :]
        ke = ke_ref[r0:r0 + GLA_CHUNK, :]
        vv = v_ref[r0:r0 + GLA_CHUNK, :]
        zk = jnp.zeros_like(kd)
        zv = jnp.zeros_like(vv)
        kbd = jnp.concatenate([jnp.where(lane_k == hh, kd, zk) for hh in range(GLA_HEADS)], axis=0)
        att = lax.dot_general(qd, kbd, nt, preferred_element_type=F32)
        att = jnp.where(tri_c <= tri_r, att, 0.0).astype(BF16)
        vbd = jnp.concatenate([jnp.where(lane_v == hh, vv, zv) for hh in range(GLA_HEADS)], axis=0)
        st = st_ref[...]
        o = _bdot(att, vbd) + lax.dot_general(qd, st.astype(BF16), nt, preferred_element_type=F32)
        oo_ref[r0:r0 + GLA_CHUNK, :] = o
        kv_t = lax.dot_general(vv, ke, tn, preferred_element_type=F32)
        decay = jnp.exp(bc_ref[r0 + GLA_CHUNK - 1:r0 + GLA_CHUNK, :])
        st_ref[...] = decay * st + jnp.where(st_r == st_c, kv_t, 0.0)

    mlp_up(6)
    mlp_up(7)
    for c in range(EV_CHUNKS):
        chunk_body(c)
        if c % 2 == 1:
            mlp_down(c // 2)

    for hh in range(GLA_HEADS):
        oh = oo_ref[:, hh * GLA_HV:(hh + 1) * GLA_HV]
        ms = jnp.mean(oh * oh, axis=-1, keepdims=True)
        gg = p_ref[:, _G + hh * GLA_HV:_G + (hh + 1) * GLA_HV]
        on = oh * lax.rsqrt(ms + RMS_EPS) * gn_ref[:, hh * GLA_HV:(hh + 1) * GLA_HV]
        mix_ref[:, RG_WIDTH + hh * GLA_HV:RG_WIDTH + (hh + 1) * GLA_HV] = (on * (gg * _sigmoid(gg))).astype(BF16)

    y = _bdot(mix_ref[...], wout_ref[...])

    def emit_out(rows, xn):
        o_ref[rows, :] = xn
    _ln_rows(zmlp_ref, lng2_ref, lnb2_ref, emit_out)
    xmid_ref[...] = ALPHA * x_ref[...] + _gate(mod_ref[...]) * y


def _even_layer_call(x, mod, mixer_params, mod_mlp, mlp_params):
    bsz, seq, _ = x.shape
    tiles_per_seq = seq // EV_T
    n_tiles = bsz * tiles_per_seq

    def mix_tile(s):
        return jnp.minimum(s, n_tiles - 1)

    def mlp_tile(s):
        return jnp.maximum(s - 1, 0)

    def tile_spec(tile_of):
        return pl.BlockSpec((None, EV_T, D_MODEL),
                            lambda s: (tile_of(s) // tiles_per_seq, tile_of(s) % tiles_per_seq, 0))

    def mod_spec(tile_of):
        return pl.BlockSpec((None, 1, 3 * D_MODEL), lambda s: (tile_of(s) // tiles_per_seq, 0, 0))

    return _tiled_call(
        functools.partial(_even_layer_kernel, tiles_per_seq=tiles_per_seq, n_tiles=n_tiles),
        grid=(n_tiles + 1,),
        in_specs=[tile_spec(mix_tile), mod_spec(mix_tile)]
        + [_resident(a.shape) for a in mixer_params] + [mod_spec(mlp_tile)]
        + [_resident(a.shape) for a in mlp_params],
        out_specs=tile_spec(mlp_tile),
        out_shape=jax.ShapeDtypeStruct((bsz, seq, D_MODEL), F32),
        scratch_shapes=[
            pltpu.VMEM((EV_T, D_MODEL), F32),
            pltpu.VMEM((EV_T, D_MODEL), F32),
            pltpu.VMEM((EV_T, D_MODEL), BF16),
            pltpu.VMEM((EV_T, D_FF), BF16),
            pltpu.VMEM((EV_T, D_MODEL), F32),
            pltpu.VMEM((EV_T, EVEN_IN_PAD), F32),
            pltpu.VMEM((EV_T, D_MODEL), BF16),
            pltpu.VMEM((SUBLANES, RG_WIDTH), F32),
            pltpu.VMEM((SUBLANES, RG_WIDTH), F32),
            pltpu.VMEM((GLA_VDIM, GLA_KDIM), F32),
            pltpu.VMEM((EV_T, GLA_KDIM), BF16),
            pltpu.VMEM((EV_T, GLA_KDIM), BF16),
            pltpu.VMEM((EV_T, GLA_KDIM), BF16),
            pltpu.VMEM((EV_T, GLA_VDIM), BF16),
            pltpu.VMEM((EV_T, GLA_KDIM), F32),
            pltpu.VMEM((EV_T, GLA_VDIM), F32),
        ],
        name="even_layer",
    )(x, mod, *mixer_params, mod_mlp, *mlp_params)


POST_TM = 512


def _post_kernel(a_ref, wout_ref, x_ref, mod_ref, lng_ref, lnb_ref,
                 mod2_ref, w1_ref, b1_ref, w2_ref, b2_ref, lng2_ref, lnb2_ref, o_ref,
                 zmix_ref, xn_ref, umlp_ref, hid_ref, zmlp_ref):
    y = _bdot(a_ref[...], wout_ref[...])
    zmix_ref[...] = ALPHA * x_ref[...] + _gate(mod_ref[...]) * y
    mlp_prepare, mlp_up, mlp_down = _mlp_pieces(zmix_ref, lng_ref, lnb_ref, mod2_ref[...], w1_ref, b1_ref,
                                                w2_ref, b2_ref, xn_ref, umlp_ref, hid_ref, zmlp_ref)
    mlp_prepare()
    for f in range(D_FF // MLP_UP):
        mlp_up(f)
    for n in range(D_MODEL // MLP_DOWN):
        mlp_down(n)

    def emit_out(rows, xn):
        o_ref[rows, :] = xn
    _ln_rows(zmlp_ref, lng2_ref, lnb2_ref, emit_out)


def _post_call(a, w_out, x, mod, ln_g, ln_b, mod_mlp, mlp_params):
    bsz, seq, _ = x.shape
    tile = pl.BlockSpec((None, POST_TM, D_MODEL), lambda b, t: (b, t, 0))
    mod_spec = pl.BlockSpec((None, 1, 3 * D_MODEL), lambda b, t: (b, 0, 0))
    return _tiled_call(
        _post_kernel,
        grid=(bsz, seq // POST_TM),
        in_specs=[tile, _resident(w_out.shape), tile, mod_spec, _resident(ln_g.shape), _resident(ln_b.shape),
                  mod_spec] + [_resident(p.shape) for p in mlp_params],
        out_specs=tile,
        out_shape=jax.ShapeDtypeStruct((bsz, seq, D_MODEL), F32),
        scratch_shapes=[
            pltpu.VMEM((POST_TM, D_MODEL), F32),
            pltpu.VMEM((POST_TM, D_MODEL), F32),
            pltpu.VMEM((POST_TM, D_MODEL), BF16),
            pltpu.VMEM((POST_TM, D_FF), BF16),
            pltpu.VMEM((POST_TM, D_MODEL), F32),
        ],
        name="post",
    )(a, w_out, x, mod, ln_g, ln_b, mod_mlp, *mlp_params)


FP_T = 512
FP_GW = 256
FP_GH = FP_GW // FOX_HD
_OQ, _OK, _OV, _OG, _OF = 0, D_MODEL, 2 * D_MODEL, 3 * D_MODEL, 4 * D_MODEL
F_ONE_COL = F_PARTS * FOX_HEADS


def _fox_place_matrix():
    place = np.zeros((LANES, FOX_HEADS * LANES), np.float32)
    for h in range(FOX_HEADS):
        base = h * LANES + (FOX_HD if h % 2 == 0 else 0)
        for part in range(F_PARTS):
            place[part * FOX_HEADS + h, base + part] = 1.0
            place[F_ONE_COL, base + F_PARTS + part] = 1.0
            place[F_ONE_COL, base + F_SLOTS + part] = 1.0
            place[part * FOX_HEADS + h, base + F_SLOTS + F_PARTS + part] = -1.0
    return place


def _fox_prep_kernel(x_ref, mod_ref, win_ref, bf_ref, gq_ref, gk_ref, place_ref,
                     qa_ref, ka_ref, vt_ref, og_ref, fc_ref):
    t_idx = pl.program_id(1)

    @pl.when(t_idx == 0)
    def _():
        fc_ref[...] = jnp.zeros_like(fc_ref)

    u_in = _modulate(x_ref[...], mod_ref[...]).astype(BF16)

    def proj(c0, width):
        return _bdot(u_in, win_ref[:, c0:c0 + width])

    lane = lax.broadcasted_iota(jnp.int32, (FP_T, LANES), 1)
    f = _log_sigmoid(proj(_OF, LANES) + bf_ref[...])
    row = lax.broadcasted_iota(jnp.int32, (FP_T, LANES), 0)
    d = 1
    while d < FP_T:
        f = jnp.where(row >= d, f + pltpu.roll(f, d, 0), f)
        d *= 2
    f = f + fc_ref[SUBLANES - 1:SUBLANES, :]
    fc_ref[...] = f[FP_T - SUBLANES:FP_T]
    f = jnp.where(lane < FOX_HEADS, f * LOG2E, 0.0)
    f1 = f.astype(BF16).astype(F32)
    r1 = f - f1
    f2 = r1.astype(BF16).astype(F32)
    f3 = (r1 - f2).astype(BF16).astype(F32)
    packed = f1 + pltpu.roll(f2, FOX_HEADS, 1) + pltpu.roll(f3, 2 * FOX_HEADS, 1)
    packed = jnp.where(lane == F_ONE_COL, 1.0, packed).astype(BF16)

    left = lane < FOX_HD

    def head_norm(xp, g):
        sq = xp * xp
        ms_l = jnp.sum(jnp.where(left, sq, 0.0), axis=-1, keepdims=True)
        ms_r = jnp.sum(jnp.where(left, 0.0, sq), axis=-1, keepdims=True)
        ms = jnp.where(left, ms_l, ms_r) * (1.0 / FOX_HD)
        return xp * lax.rsqrt(ms + RMS_EPS) * g

    def group_products(gi):
        c0 = gi * FP_GW
        h0 = c0 // FOX_HD
        return (proj(_OQ + c0, FP_GW), proj(_OK + c0, FP_GW), proj(_OV + c0, FP_GW), proj(_OG + c0, FP_GW),
                _bdot(packed, place_ref[:, h0 * LANES:(h0 + FP_GH) * LANES]))

    def group_finish(gi, prods):
        qg, kg, vg, gg, feat = prods
        c0 = gi * FP_GW
        og_ref[:, c0:c0 + FP_GW] = _sigmoid(gg).astype(BF16)
        for pi in range(FP_GW // LANES):
            l0 = pi * LANES
            qn = head_norm(qg[:, l0:l0 + LANES], gq_ref[...]) * (FOX_HD ** -0.5 * LOG2E)
            kn = head_norm(kg[:, l0:l0 + LANES], gk_ref[...])
            vv = vg[:, l0:l0 + LANES]
            for e in range(2):
                hl = 2 * pi + e
                h = gi * FP_GH + hl
                own = left if e == 0 else jnp.logical_not(left)
                fh = feat[:, hl * LANES:(hl + 1) * LANES]
                qa_ref[h] = jnp.where(own, qn, fh).astype(BF16)
                ka_ref[h] = jnp.where(own, kn, pltpu.roll(fh, LANES - F_SLOTS, 1)).astype(BF16)
                vt_ref[h, 0] = jnp.where(own, vv, 1.0).T.astype(BF16)

    n_groups = D_MODEL // FP_GW
    pending = group_products(0)
    for gi in range(n_groups):
        nxt = group_products(gi + 1) if gi + 1 < n_groups else None
        group_finish(gi, pending)
        pending = nxt


def _fox_prep_call(x, mod, w_in, b_f, gq, gk, place):
    bsz, seq, _ = x.shape
    small = [w_in, b_f, gq, gk, place]
    return _tiled_call(
        _fox_prep_kernel,
        grid=(bsz, seq // FP_T),
        in_specs=[pl.BlockSpec((None, FP_T, D_MODEL), lambda b, t: (b, t, 0)),
                  pl.BlockSpec((None, 1, 3 * D_MODEL), lambda b, t: (b, 0, 0))]
        + [_resident(a.shape) for a in small],
        out_specs=[
            pl.BlockSpec((None, FOX_HEADS, FP_T, LANES), lambda b, t: (b, 0, t, 0)),
            pl.BlockSpec((None, FOX_HEADS, FP_T, LANES), lambda b, t: (b, 0, t, 0)),
            pl.BlockSpec((None, FOX_HEADS, 1, LANES, FP_T), lambda b, t: (b, 0, t, 0, 0)),
            pl.BlockSpec((None, FP_T, D_MODEL), lambda b, t: (b, t, 0)),
        ],
        out_shape=[
            jax.ShapeDtypeStruct((bsz, FOX_HEADS, seq, LANES), BF16),
            jax.ShapeDtypeStruct((bsz, FOX_HEADS, seq, LANES), BF16),
            jax.ShapeDtypeStruct((bsz, FOX_HEADS, seq // FP_T, LANES, FP_T), BF16),
            jax.ShapeDtypeStruct((bsz, seq, D_MODEL), BF16),
        ],
        scratch_shapes=[pltpu.VMEM((SUBLANES, LANES), F32)],
        name="fox_prep",
    )(x, mod, *small)


FA_T = FP_T
FA_NH = 4
FA_NP = FA_NH // 2
FA_GROUP = 3


def _fox_attn_kernel(qa_ref, ka_ref, vt_ref, g_ref, o_ref, m_ref, acc_ref, *s_refs):
    seq = qa_ref.shape[1]
    n_q = seq // FA_T
    n_items = n_q * (n_q + 1) // 2
    nt = (((1,), (1,)), ((), ()))
    key = lax.broadcasted_iota(jnp.int32, (FA_T, FA_T), 0)
    qry = lax.broadcasted_iota(jnp.int32, (FA_T, FA_T), 1)
    top = lax.broadcasted_iota(jnp.int32, (LANES, FA_T), 0) < FOX_HD

    def reset_stats():
        m_ref[...] = jnp.full_like(m_ref, MASK_NEG)
        acc_ref[...] = jnp.zeros_like(acc_ref)

    def scores(qi, j, dst_ref, h):
        q0 = pl.multiple_of(qi * FA_T, FA_T)
        k0 = pl.multiple_of(j * FA_T, FA_T)
        dst_ref[h] = lax.dot_general(ka_ref[h, pl.ds(k0, FA_T), :], qa_ref[h, pl.ds(q0, FA_T), :], nt,
                                     preferred_element_type=F32)

    def softmax_pv(j, src_ref, h, masked):
        s_t = src_ref[h]
        if masked:
            s_t = jnp.where(key <= qry, s_t, MASK_NEG)
        m_prev = m_ref[h]
        m_new = jnp.maximum(m_prev, jnp.max(s_t, axis=0, keepdims=True))
        p_t = jnp.exp2(s_t - m_new).astype(BF16)
        acc_ref[h] = jnp.exp2(m_prev - m_new) * acc_ref[h] + _bdot(vt_ref[h, j], p_t)
        m_ref[h] = m_new

    def finish_tile(qi):
        q0 = pl.multiple_of(qi * FA_T, FA_T)
        for p in range(FA_NP):
            acc0 = acc_ref[2 * p]
            acc1 = acc_ref[2 * p + 1]
            o_t = jnp.where(top, acc0 * (1.0 / acc0[FOX_HD:FOX_HD + 1, :]), acc1 * (1.0 / acc1[0:1, :]))
            g = g_ref[pl.ds(q0, FA_T), p * LANES:(p + 1) * LANES].astype(F32)
            o_ref[pl.ds(q0, FA_T), p * LANES:(p + 1) * LANES] = (o_t.T * g).astype(BF16)
        reset_stats()

    def advance(qi, j):
        diag = j == qi
        return diag, jnp.where(diag, jnp.minimum(qi + 1, n_q - 1), qi), jnp.where(diag, 0, j + 1)

    def item(qi, j, cur_ref, masked, nxt, nxt_ref):
        for h in range(FA_NH):
            scores(nxt[0], nxt[1], nxt_ref, h)
            softmax_pv(j, cur_ref, h, masked)
        if masked:
            finish_tile(qi)

    reset_stats()
    for h in range(FA_NH):
        scores(0, 0, s_refs[0], h)

    patterns = [p for p in itertools.product((False, True), repeat=FA_GROUP)
                if not any(p[k] and p[k + 1] for k in range(FA_GROUP - 1))]

    def group_body(i, carry):
        qs, js, diags = [carry[0]], [carry[1]], []
        for _ in range(FA_GROUP):
            diag, q_n, j_n = advance(qs[-1], js[-1])
            diags.append(diag)
            qs.append(q_n)
            js.append(j_n)

        def run(pattern):
            for k in range(FA_GROUP):
                item(qs[k], js[k], s_refs[k % len(s_refs)], pattern[k], (qs[k + 1], js[k + 1]),
                     s_refs[(k + 1) % len(s_refs)])

        for pattern in patterns:
            cond = functools.reduce(jnp.logical_and,
                                    [d if m else jnp.logical_not(d) for d, m in zip(diags, pattern)])
            pl.when(cond)(functools.partial(run, pattern))
        return qs[-1], js[-1]

    assert n_q >= 2 and n_items % FA_GROUP == 0 and FA_GROUP % len(s_refs) == 0
    lax.fori_loop(0, n_items // FA_GROUP, group_body, (jnp.int32(0), jnp.int32(0)))


def _fox_attn_call(qa, ka, vt, og):
    bsz, _, seq, _ = qa.shape
    gw = FA_NP * LANES
    return _tiled_call(
        _fox_attn_kernel,
        grid=(bsz, FOX_HEADS // FA_NH),
        in_specs=[
            pl.BlockSpec((None, FA_NH, seq, LANES), lambda b, p: (b, p, 0, 0)),
            pl.BlockSpec((None, FA_NH, seq, LANES), lambda b, p: (b, p, 0, 0)),
            pl.BlockSpec((None, FA_NH, seq // FA_T, LANES, FA_T), lambda b, p: (b, p, 0, 0, 0)),
            pl.BlockSpec((None, seq, gw), lambda b, p: (b, 0, p)),
        ],
        out_specs=pl.BlockSpec((None, seq, gw), lambda b, p: (b, 0, p)),
        out_shape=jax.ShapeDtypeStruct((bsz, seq, D_MODEL), BF16),
        scratch_shapes=[
            pltpu.VMEM((FA_NH, 1, FA_T), F32),
            pltpu.VMEM((FA_NH, LANES, FA_T), F32),
        ] + [pltpu.VMEM((FA_NH, FA_T, FA_T), F32)] * FA_GROUP,
        name="fox_attn",
    )(qa, ka, vt, og)


def _pad_cols(w, n):
    return jnp.pad(w, ((0, 0), (0, n - w.shape[1])))


def _gate_weights(w_a, w_x):
    per_half = RG_BLOCKS // 2
    half = RG_WIDTH // 2

    def bd(w, j):
        m = jnp.zeros((half, half), F32)
        for i in range(per_half):
            m = lax.dynamic_update_slice(m, w[j * per_half + i], (i * RG_BLOCK, i * RG_BLOCK))
        return m

    return jnp.stack([jnp.concatenate([bd(w_a, j), bd(w_x, j)], axis=1) for j in range(2)]).astype(BF16)


def kernel(x, c, ada_w, ada_b, ln_g, ln_b, ev_w_in, ev_conv_w, ev_conv_b, ev_rg_wa, ev_rg_ba, ev_rg_wx,
           ev_rg_bx, ev_rg_lam, ev_gla_w_up, ev_gla_b_up, ev_gla_norm_g, ev_w_out, od_w_in, od_b_f,
           od_q_norm_g, od_k_norm_g, od_w_out, mlp_w1, mlp_b1, mlp_w2, mlp_b2):
    bsz = x.shape[0]
    mods = _ada_call(c, ada_w.reshape(2 * DEPTH, D_MODEL, 3 * D_MODEL),
                     ada_b.reshape(2 * DEPTH, 1, 3 * D_MODEL))
    mods = mods.reshape(2 * DEPTH, bsz, 1, 3 * D_MODEL)
    place = jnp.asarray(_fox_place_matrix(), BF16)

    w1_all = mlp_w1.astype(BF16)
    w2_all = mlp_w2.astype(BF16)
    for layer in range(DEPTH):
        mod_mix = mods[2 * layer]
        mod_mlp = mods[2 * layer + 1]
        mlp_params = [w1_all[layer], mlp_b1[layer][None], w2_all[layer],
                      mlp_b2[layer][None], ln_g[layer, 1][None], ln_b[layer, 1][None]]
        if layer % 2 == 0:
            e = layer // 2
            w_up = jnp.pad(ev_gla_w_up[e], ((0, LANES - GLA_LOWRANK), (0, 0)))
            mixer_params = [
                _pad_cols(ev_w_in[e].astype(BF16), EVEN_IN_PAD), ev_conv_w[e], ev_conv_b[e][None],
                _gate_weights(ev_rg_wa[e], ev_rg_wx[e]), ev_rg_ba[e][None], ev_rg_bx[e][None],
                ev_rg_lam[e][None], w_up, ev_gla_b_up[e][None], ev_gla_norm_g[e][None],
                ev_w_out[e].astype(BF16), ln_g[layer, 0][None], ln_b[layer, 0][None]]
            x = _even_layer_call(x, mod_mix, mixer_params, mod_mlp, mlp_params)
        else:
            o = layer // 2
            w_in = _pad_cols(od_w_in[o].astype(BF16), ODD_IN_PAD)
            b_f = jnp.pad(od_b_f[o], (0, LANES - FOX_HEADS))[None]
            gq = jnp.tile(od_q_norm_g[o], 2)[None]
            gk = jnp.tile(od_k_norm_g[o], 2)[None]
            qa, ka, vt, og = _fox_prep_call(x, mod_mix, w_in, b_f, gq, gk, place)
            mix = _fox_attn_call(qa, ka, vt, og)
            x = _post_call(mix, od_w_out[o].astype(BF16), x, mod_mix, ln_g[layer, 0][None],
                           ln_b[layer, 0][None], mod_mlp, mlp_params)
    return x
```

```python
import functools
import itertools

import numpy as np
import jax
import jax.numpy as jnp
from jax import lax
from jax.experimental import pallas as pl
from jax.experimental.pallas import tpu as pltpu

F32 = jnp.float32
BF16 = jnp.bfloat16

D_MODEL = 1024
DEPTH = 2
RG_WIDTH = D_MODEL // 2
RG_BLOCKS = 8
RG_BLOCK = RG_WIDTH // RG_BLOCKS
CONV_WIDTH = 4
RG_C = 8.0
GLA_HEADS = 4
GLA_VDIM = D_MODEL // 2
GLA_KDIM = GLA_VDIM // 2
GLA_HK = GLA_KDIM // GLA_HEADS
GLA_HV = GLA_VDIM // GLA_HEADS
GLA_LOWRANK = 16
GLA_GATE_TAU = 16.0
GLA_CHUNK = 64
EVEN_IN = 2 * RG_WIDTH + 2 * GLA_KDIM + 2 * GLA_VDIM + GLA_LOWRANK
FOX_HEADS = 16
FOX_HD = D_MODEL // FOX_HEADS
FOX_PAIRS = FOX_HEADS // 2
ODD_IN = 4 * D_MODEL + FOX_HEADS
D_FF = 4 * D_MODEL
ALPHA = (2 * DEPTH) ** 0.25
LN_EPS = 1e-5
RMS_EPS = 1e-6

LANES = 128
SUBLANES = 8
VMEM_BYTES = 64 * 1024 * 1024
VMEM_CAP = VMEM_BYTES - 8 * 1024 * 1024
VMEM_TEMP = 16 * 1024 * 1024

EVEN_IN_PAD = EVEN_IN + (LANES - GLA_LOWRANK)
ODD_IN_PAD = ODD_IN + (LANES - FOX_HEADS)
MASK_NEG = -1e30
LOG2E = float(np.log2(np.e))

F_PARTS = 3
F_SLOTS = 8

PROJ_TN = 512


def _tiled_call(body, *, grid, in_specs, out_specs, out_shape, scratch_shapes=(), name):
    out_specs_l = list(out_specs) if isinstance(out_specs, (list, tuple)) else [out_specs]
    out_shape_l = list(out_shape) if isinstance(out_shape, (list, tuple)) else [out_shape]

    def run(*operands):
        need = VMEM_TEMP
        for spec, arr in zip(list(in_specs) + out_specs_l, list(operands) + out_shape_l):
            n_buf = 2 if spec.pipeline_mode is None else 1
            block = [d for d in spec.block_shape if d is not None]
            need += n_buf * int(np.prod(block)) * jnp.dtype(arr.dtype).itemsize
        for ref in scratch_shapes:
            need += int(np.prod(ref.shape)) * jnp.dtype(ref.dtype).itemsize
        params = pltpu.CompilerParams(dimension_semantics=("arbitrary",) * len(grid),
                                      vmem_limit_bytes=min(need, VMEM_CAP))
        return pl.pallas_call(body, grid=grid, in_specs=in_specs, out_specs=out_specs, out_shape=out_shape,
                              scratch_shapes=list(scratch_shapes), compiler_params=params, name=name)(*operands)

    return run


def _resident(shape):
    nd = len(shape)
    return pl.BlockSpec(shape, lambda *_: (0,) * nd, pipeline_mode=pl.Buffered(1))


def _sigmoid(x):
    return 0.5 * jnp.tanh(0.5 * x) + 0.5


def _log_sigmoid(x):
    return jnp.minimum(x, 0.0) - jnp.log1p(jnp.exp(-jnp.abs(x)))


def _gelu_tanh(x):
    c = np.sqrt(2.0 / np.pi).astype(np.float32)
    return x * (0.5 * (1.0 + jnp.tanh(c * (x + 0.044715 * (x * x * x)))))


def _sqrt_nonneg(y):
    return jnp.where(y > 0.0, y * lax.rsqrt(y), 0.0)


def _bdot(a, b):
    return jnp.dot(a, b, preferred_element_type=F32)


def _split2(a):
    hi = a.astype(BF16)
    lo = (a - hi.astype(F32)).astype(BF16)
    return hi, lo


def _dot_split(a, w):
    ah, al = _split2(a)
    wh, wl = _split2(w)
    return _bdot(ah, wh) + (_bdot(ah, wl) + _bdot(al, wh))


def _layer_norm(z, g, b):
    mu = jnp.mean(z, axis=-1, keepdims=True)
    zc = z - mu
    var = jnp.mean(zc * zc, axis=-1, keepdims=True)
    return zc * lax.rsqrt(var + LN_EPS) * g + b


LN_ROWS = 64


def _ln_rows(src_ref, g_ref, b_ref, emit):
    g = g_ref[...]
    b = b_ref[...]
    for r0 in range(0, src_ref.shape[0], LN_ROWS):
        rows = slice(r0, r0 + LN_ROWS)
        emit(rows, _layer_norm(src_ref[rows, :], g, b))


def _modulate(x, mod):
    shift = mod[:, 0:D_MODEL]
    scale = mod[:, D_MODEL:2 * D_MODEL]
    return x * (1.0 + scale) + shift


def _gate(mod):
    return 1.0 + mod[:, 2 * D_MODEL:3 * D_MODEL]


MLP_UP = 512
MLP_DOWN = 256


def _mlp_pieces(src_ref, g_ref, b_ref, mod, w1_ref, b1_ref, w2_ref, b2_ref, xn_ref, u_ref, hid_ref, z_ref):
    gate = _gate(mod)

    def prepare():
        def emit(rows, xn):
            xn_ref[rows, :] = xn
            u_ref[rows, :] = _modulate(xn, mod).astype(BF16)
        _ln_rows(src_ref, g_ref, b_ref, emit)

    def up(f):
        f0 = f * MLP_UP
        hdn = _bdot(u_ref[...], w1_ref[:, f0:f0 + MLP_UP]) + b1_ref[:, f0:f0 + MLP_UP]
        hdn = jnp.maximum(hdn, 0.0)
        hid_ref[:, f0:f0 + MLP_UP] = (hdn * hdn).astype(BF16)

    def down(n):
        n0 = n * MLP_DOWN
        y = _bdot(hid_ref[...], w2_ref[:, n0:n0 + MLP_DOWN]) + b2_ref[:, n0:n0 + MLP_DOWN]
        z_ref[:, n0:n0 + MLP_DOWN] = ALPHA * xn_ref[:, n0:n0 + MLP_DOWN] + gate[:, n0:n0 + MLP_DOWN] * y

    return prepare, up, down


ADA_TN = 1024


def _ada_kernel(c_ref, w_ref, b_ref, o_ref):
    c = c_ref[...]
    s = c * _sigmoid(c)
    o_ref[...] = _dot_split(s, w_ref[...]) + b_ref[...]


def _ada_call(c, ada_w, ada_b):
    n_mod = ada_w.shape[0]
    bsz = c.shape[0]
    return _tiled_call(
        _ada_kernel,
        grid=(n_mod, 3 * D_MODEL // ADA_TN),
        in_specs=[
            pl.BlockSpec((bsz, D_MODEL), lambda i, j: (0, 0)),
            pl.BlockSpec((None, D_MODEL, ADA_TN), lambda i, j: (i, 0, j)),
            pl.BlockSpec((None, 1, ADA_TN), lambda i, j: (i, 0, j)),
        ],
        out_specs=pl.BlockSpec((None, bsz, ADA_TN), lambda i, j: (i, 0, j)),
        out_shape=jax.ShapeDtypeStruct((n_mod, bsz, 3 * D_MODEL), F32),
        name="ada",
    )(c, ada_w, ada_b)


EV_T = 512
EV_CHUNKS = EV_T // GLA_CHUNK
_XR, _YR = 0, RG_WIDTH
_Q = 2 * RG_WIDTH
_K = _Q + GLA_KDIM
_V = _K + GLA_KDIM
_G = _V + GLA_VDIM
_ZL = _G + GLA_VDIM


def _scan_rows8(a, u):
    rows, ch = a.shape
    a3 = a.reshape(rows // SUBLANES, SUBLANES, ch)
    u3 = u.reshape(rows // SUBLANES, SUBLANES, ch)
    sub = lax.broadcasted_iota(jnp.int32, a3.shape, 1)
    d = 1
    while d < SUBLANES:
        keep = sub >= d
        u3 = jnp.where(keep, a3 * pltpu.roll(u3, d, 1) + u3, u3)
        a3 = jnp.where(keep, a3 * pltpu.roll(a3, d, 1), a3)
        d *= 2
    return a3, u3


def _even_layer_kernel(x_ref, mod_ref, win_ref, convw_ref, convb_ref, wg_ref, ba_ref, bx_ref, lam_ref,
                       wup_ref, bup_ref, gn_ref, wout_ref, lng_ref, lnb_ref,
                       mod2_ref, w1_ref, b1_ref, w2_ref, b2_ref, lng2_ref, lnb2_ref, o_ref,
                       xmid_ref, xn_ref, umlp_ref, hid_ref, zmlp_ref, p_ref, mix_ref, tail_ref, hc_ref, st_ref,
                       qd_ref, kd_ref, ke_ref, v_ref, bc_ref, oo_ref, *, tiles_per_seq, n_tiles):
    step = pl.program_id(0)
    t_idx = jnp.minimum(step, n_tiles - 1) % tiles_per_seq

    @pl.when(step == 0)
    def _():
        xmid_ref[...] = jnp.zeros_like(xmid_ref)

    @pl.when(t_idx == 0)
    def _():
        tail_ref[...] = jnp.zeros_like(tail_ref)
        hc_ref[...] = jnp.zeros_like(hc_ref)
        st_ref[...] = jnp.zeros_like(st_ref)

    mlp_prepare, mlp_up, mlp_down = _mlp_pieces(xmid_ref, lng_ref, lnb_ref, mod2_ref[...], w1_ref, b1_ref,
                                                w2_ref, b2_ref, xn_ref, umlp_ref, hid_ref, zmlp_ref)
    mlp_prepare()

    u_in = _modulate(x_ref[...], mod_ref[...]).astype(BF16)
    for c0 in range(0, EVEN_IN_PAD, PROJ_TN):
        c1 = min(c0 + PROJ_TN, EVEN_IN_PAD)
        p_ref[:, c0:c1] = _bdot(u_in, win_ref[:, c0:c1])

    xr = p_ref[:, _XR:_XR + RG_WIDTH]
    tail = tail_ref[...]
    row8 = lax.broadcasted_iota(jnp.int32, (SUBLANES, RG_WIDTH), 0)
    xc = convb_ref[...] + convw_ref[CONV_WIDTH - 1:CONV_WIDTH, :] * xr
    for j in range(1, CONV_WIDTH):
        xs = pltpu.roll(xr, j, 0)
        head = jnp.where(row8 < j, pltpu.roll(tail, j, 0), xs[0:SUBLANES])
        xs = jnp.concatenate([head, xs[SUBLANES:]], axis=0)
        xc = xc + convw_ref[CONV_WIDTH - 1 - j:CONV_WIDTH - j, :] * xs
    tail_ref[...] = xr[EV_T - SUBLANES:EV_T]

    half = RG_WIDTH // 2
    r_parts, i_parts = [], []
    for j in range(2):
        ri = _bdot(xc[:, j * half:(j + 1) * half].astype(BF16), wg_ref[j])
        r_parts.append(ri[:, :half])
        i_parts.append(ri[:, half:])
    for f in range(3):
        mlp_up(f)
    r = _sigmoid(jnp.concatenate(r_parts, axis=1) + ba_ref[...])
    ig = _sigmoid(jnp.concatenate(i_parts, axis=1) + bx_ref[...])
    log_a = RG_C * r * _log_sigmoid(lam_ref[...])
    a = jnp.exp(log_a)
    u = _sqrt_nonneg(-jnp.tanh(log_a) * (a * a + 1.0)) * (ig * xc)

    a3, h3 = _scan_rows8(a, u)
    carry = jnp.broadcast_to(hc_ref[SUBLANES - 1:SUBLANES, :], (SUBLANES, RG_WIDTH))
    h_groups = []
    n_groups = EV_T // SUBLANES
    for g in range(n_groups):
        hg = h3[g] + a3[g] * carry
        h_groups.append(hg)
        carry = jnp.broadcast_to(hg[SUBLANES - 1:SUBLANES, :], (SUBLANES, RG_WIDTH))
        if g == n_groups // 3:
            mlp_up(3)
        if g == 2 * n_groups // 3:
            mlp_up(4)
    hc_ref[...] = carry
    h = jnp.concatenate(h_groups, axis=0)
    mix_ref[:, 0:RG_WIDTH] = (h * _gelu_tanh(p_ref[:, _YR:_YR + RG_WIDTH])).astype(BF16)
    mlp_up(5)

    z = _dot_split(p_ref[:, _ZL:_ZL + LANES], wup_ref[...]) + bup_ref[...]
    bc = _log_sigmoid(z) * (1.0 / GLA_GATE_TAU)
    rowk = lax.broadcasted_iota(jnp.int32, (EV_T, GLA_KDIM), 0) & (GLA_CHUNK - 1)
    d = 1
    while d < GLA_CHUNK:
        bc = jnp.where(rowk >= d, bc + pltpu.roll(bc, d, 0), bc)
        d *= 2
    bc3 = bc.reshape(EV_CHUNKS, GLA_CHUNK, GLA_KDIM)
    b_last = jnp.broadcast_to(bc3[:, GLA_CHUNK - 1:GLA_CHUNK, :], bc3.shape).reshape(EV_T, GLA_KDIM)
    q = p_ref[:, _Q:_Q + GLA_KDIM]
    k = p_ref[:, _K:_K + GLA_KDIM]
    qd_ref[...] = (q * (GLA_HK ** -0.5) * jnp.exp(bc)).astype(BF16)
    kd_ref[...] = (k * jnp.exp(-bc)).astype(BF16)
    ke_ref[...] = (k * jnp.exp(b_last - bc)).astype(BF16)
    v_ref[...] = p_ref[:, _V:_V + GLA_VDIM].astype(BF16)
    bc_ref[...] = bc

    lane_k = lax.broadcasted_iota(jnp.int32, (GLA_CHUNK, GLA_KDIM), 1) // GLA_HK
    lane_v = lax.broadcasted_iota(jnp.int32, (GLA_CHUNK, GLA_VDIM), 1) // GLA_HV
    tri_r = lax.broadcasted_iota(jnp.int32, (GLA_CHUNK, GLA_HEADS * GLA_CHUNK), 0)
    tri_c = lax.broadcasted_iota(jnp.int32, (GLA_CHUNK, GLA_HEADS * GLA_CHUNK), 1) & (GLA_CHUNK - 1)
    st_r = lax.broadcasted_iota(jnp.int32, (GLA_VDIM, GLA_KDIM), 0) // GLA_HV
    st_c = lax.broadcasted_iota(jnp.int32, (GLA_VDIM, GLA_KDIM), 1) // GLA_HK
    nt = (((1,), (1,)), ((), ()))
    tn = (((0,), (0,)), ((), ()))

    def chunk_body(c):
        r0 = c * GLA_CHUNK
        qd = qd_ref[r0:r0 + GLA_CHUNK, :]
        kd = kd_ref[r0:r0 + GLA_CHUNK, :]
        ke = ke_ref[r0:r0 + GLA_CHUNK, :]
        vv = v_ref[r0:r0 + GLA_CHUNK, :]
        zk = jnp.zeros_like(kd)
        zv = jnp.zeros_like(vv)
        kbd = jnp.concatenate([jnp.where(lane_k == hh, kd, zk) for hh in range(GLA_HEADS)], axis=0)
        att = lax.dot_general(qd, kbd, nt, preferred_element_type=F32)
        att = jnp.where(tri_c <= tri_r, att, 0.0).astype(BF16)
        vbd = jnp.concatenate([jnp.where(lane_v == hh, vv, zv) for hh in range(GLA_HEADS)], axis=0)
        st = st_ref[...]
        o = _bdot(att, vbd) + lax.dot_general(qd, st.astype(BF16), nt, preferred_element_type=F32)
        oo_ref[r0:r0 + GLA_CHUNK, :] = o
        kv_t = lax.dot_general(vv, ke, tn, preferred_element_type=F32)
        decay = jnp.exp(bc_ref[r0 + GLA_CHUNK - 1:r0 + GLA_CHUNK, :])
        st_ref[...] = decay * st + jnp.where(st_r == st_c, kv_t, 0.0)

    mlp_up(6)
    mlp_up(7)
    for c in range(EV_CHUNKS):
        chunk_body(c)
        if c % 2 == 1:
            mlp_down(c // 2)

    for hh in range(GLA_HEADS):
        oh = oo_ref[:, hh * GLA_HV:(hh + 1) * GLA_HV]
        ms = jnp.mean(oh * oh, axis=-1, keepdims=True)
        gg = p_ref[:, _G + hh * GLA_HV:_G + (hh + 1) * GLA_HV]
        on = oh * lax.rsqrt(ms + RMS_EPS) * gn_ref[:, hh * GLA_HV:(hh + 1) * GLA_HV]
        mix_ref[:, RG_WIDTH + hh * GLA_HV:RG_WIDTH + (hh + 1) * GLA_HV] = (on * (gg * _sigmoid(gg))).astype(BF16)

    y = _bdot(mix_ref[...], wout_ref[...])

    def emit_out(rows, xn):
        o_ref[rows, :] = xn
    _ln_rows(zmlp_ref, lng2_ref, lnb2_ref, emit_out)
    xmid_ref[...] = ALPHA * x_ref[...] + _gate(mod_ref[...]) * y


def _even_layer_call(x, mod, mixer_params, mod_mlp, mlp_params):
    bsz, seq, _ = x.shape
    tiles_per_seq = seq // EV_T
    n_tiles = bsz * tiles_per_seq

    def mix_tile(s):
        return jnp.minimum(s, n_tiles - 1)

    def mlp_tile(s):
        return jnp.maximum(s - 1, 0)

    def tile_spec(tile_of):
        return pl.BlockSpec((None, EV_T, D_MODEL),
                            lambda s: (tile_of(s) // tiles_per_seq, tile_of(s) % tiles_per_seq, 0))

    def mod_spec(tile_of):
        return pl.BlockSpec((None, 1, 3 * D_MODEL), lambda s: (tile_of(s) // tiles_per_seq, 0, 0))

    return _tiled_call(
        functools.partial(_even_layer_kernel, tiles_per_seq=tiles_per_seq, n_tiles=n_tiles),
        grid=(n_tiles + 1,),
        in_specs=[tile_spec(mix_tile), mod_spec(mix_tile)]
        + [_resident(a.shape) for a in mixer_params] + [mod_spec(mlp_tile)]
        + [_resident(a.shape) for a in mlp_params],
        out_specs=tile_spec(mlp_tile),
        out_shape=jax.ShapeDtypeStruct((bsz, seq, D_MODEL), F32),
        scratch_shapes=[
            pltpu.VMEM((EV_T, D_MODEL), F32),
            pltpu.VMEM((EV_T, D_MODEL), F32),
            pltpu.VMEM((EV_T, D_MODEL), BF16),
            pltpu.VMEM((EV_T, D_FF), BF16),
            pltpu.VMEM((EV_T, D_MODEL), F32),
            pltpu.VMEM((EV_T, EVEN_IN_PAD), F32),
            pltpu.VMEM((EV_T, D_MODEL), BF16),
            pltpu.VMEM((SUBLANES, RG_WIDTH), F32),
            pltpu.VMEM((SUBLANES, RG_WIDTH), F32),
            pltpu.VMEM((GLA_VDIM, GLA_KDIM), F32),
            pltpu.VMEM((EV_T, GLA_KDIM), BF16),
            pltpu.VMEM((EV_T, GLA_KDIM), BF16),
            pltpu.VMEM((EV_T, GLA_KDIM), BF16),
            pltpu.VMEM((EV_T, GLA_VDIM), BF16),
            pltpu.VMEM((EV_T, GLA_KDIM), F32),
            pltpu.VMEM((EV_T, GLA_VDIM), F32),
        ],
        name="even_layer",
    )(x, mod, *mixer_params, mod_mlp, *mlp_params)


POST_TM = 512


def _post_kernel(a_ref, wout_ref, x_ref, mod_ref, lng_ref, lnb_ref,
                 mod2_ref, w1_ref, b1_ref, w2_ref, b2_ref, lng2_ref, lnb2_ref, o_ref,
                 zmix_ref, xn_ref, umlp_ref, hid_ref, zmlp_ref):
    y = _bdot(a_ref[...], wout_ref[...])
    zmix_ref[...] = ALPHA * x_ref[...] + _gate(mod_ref[...]) * y
    mlp_prepare, mlp_up, mlp_down = _mlp_pieces(zmix_ref, lng_ref, lnb_ref, mod2_ref[...], w1_ref, b1_ref,
                                                w2_ref, b2_ref, xn_ref, umlp_ref, hid_ref, zmlp_ref)
    mlp_prepare()
    for f in range(D_FF // MLP_UP):
        mlp_up(f)
    for n in range(D_MODEL // MLP_DOWN):
        mlp_down(n)

    def emit_out(rows, xn):
        o_ref[rows, :] = xn
    _ln_rows(zmlp_ref, lng2_ref, lnb2_ref, emit_out)


def _post_call(a, w_out, x, mod, ln_g, ln_b, mod_mlp, mlp_params):
    bsz, seq, _ = x.shape
    tile = pl.BlockSpec((None, POST_TM, D_MODEL), lambda b, t: (b, t, 0))
    mod_spec = pl.BlockSpec((None, 1, 3 * D_MODEL), lambda b, t: (b, 0, 0))
    return _tiled_call(
        _post_kernel,
        grid=(bsz, seq // POST_TM),
        in_specs=[tile, _resident(w_out.shape), tile, mod_spec, _resident(ln_g.shape), _resident(ln_b.shape),
                  mod_spec] + [_resident(p.shape) for p in mlp_params],
        out_specs=tile,
        out_shape=jax.ShapeDtypeStruct((bsz, seq, D_MODEL), F32),
        scratch_shapes=[
            pltpu.VMEM((POST_TM, D_MODEL), F32),
            pltpu.VMEM((POST_TM, D_MODEL), F32),
            pltpu.VMEM((POST_TM, D_MODEL), BF16),
            pltpu.VMEM((POST_TM, D_FF), BF16),
            pltpu.VMEM((POST_TM, D_MODEL), F32),
        ],
        name="post",
    )(a, w_out, x, mod, ln_g, ln_b, mod_mlp, *mlp_params)


FP_T = 512
FP_GW = 256
FP_GH = FP_GW // FOX_HD
_OQ, _OK, _OV, _OG, _OF = 0, D_MODEL, 2 * D_MODEL, 3 * D_MODEL, 4 * D_MODEL
F_ONE_COL = F_PARTS * FOX_HEADS


def _fox_place_matrix():
    place = np.zeros((LANES, FOX_HEADS * LANES), np.float32)
    for h in range(FOX_HEADS):
        base = h * LANES + (FOX_HD if h % 2 == 0 else 0)
        for part in range(F_PARTS):
            place[part * FOX_HEADS + h, base + part] = 1.0
            place[F_ONE_COL, base + F_PARTS + part] = 1.0
            place[F_ONE_COL, base + F_SLOTS + part] = 1.0
            place[part * FOX_HEADS + h, base + F_SLOTS + F_PARTS + part] = -1.0
    return place


def _fox_prep_kernel(x_ref, mod_ref, win_ref, bf_ref, gq_ref, gk_ref, place_ref,
                     qa_ref, ka_ref, vt_ref, og_ref, fc_ref):
    t_idx = pl.program_id(1)

    @pl.when(t_idx == 0)
    def _():
        fc_ref[...] = jnp.zeros_like(fc_ref)

    u_in = _modulate(x_ref[...], mod_ref[...]).astype(BF16)

    def proj(c0, width):
        return _bdot(u_in, win_ref[:, c0:c0 + width])

    lane = lax.broadcasted_iota(jnp.int32, (FP_T, LANES), 1)
    f = _log_sigmoid(proj(_OF, LANES) + bf_ref[...])
    row = lax.broadcasted_iota(jnp.int32, (FP_T, LANES), 0)
    d = 1
    while d < FP_T:
        f = jnp.where(row >= d, f + pltpu.roll(f, d, 0), f)
        d *= 2
    f = f + fc_ref[SUBLANES - 1:SUBLANES, :]
    fc_ref[...] = f[FP_T - SUBLANES:FP_T]
    f = jnp.where(lane < FOX_HEADS, f * LOG2E, 0.0)
    f1 = f.astype(BF16).astype(F32)
    r1 = f - f1
    f2 = r1.astype(BF16).astype(F32)
    f3 = (r1 - f2).astype(BF16).astype(F32)
    packed = f1 + pltpu.roll(f2, FOX_HEADS, 1) + pltpu.roll(f3, 2 * FOX_HEADS, 1)
    packed = jnp.where(lane == F_ONE_COL, 1.0, packed).astype(BF16)

    left = lane < FOX_HD

    def head_norm(xp, g):
        sq = xp * xp
        ms_l = jnp.sum(jnp.where(left, sq, 0.0), axis=-1, keepdims=True)
        ms_r = jnp.sum(jnp.where(left, 0.0, sq), axis=-1, keepdims=True)
        ms = jnp.where(left, ms_l, ms_r) * (1.0 / FOX_HD)
        return xp * lax.rsqrt(ms + RMS_EPS) * g

    def group_products(gi):
        c0 = gi * FP_GW
        h0 = c0 // FOX_HD
        return (proj(_OQ + c0, FP_GW), proj(_OK + c0, FP_GW), proj(_OV + c0, FP_GW), proj(_OG + c0, FP_GW),
                _bdot(packed, place_ref[:, h0 * LANES:(h0 + FP_GH) * LANES]))

    def group_finish(gi, prods):
        qg, kg, vg, gg, feat = prods
        c0 = gi * FP_GW
        og_ref[:, c0:c0 + FP_GW] = _sigmoid(gg).astype(BF16)
        for pi in range(FP_GW // LANES):
            l0 = pi * LANES
            qn = head_norm(qg[:, l0:l0 + LANES], gq_ref[...]) * (FOX_HD ** -0.5 * LOG2E)
            kn = head_norm(kg[:, l0:l0 + LANES], gk_ref[...])
            vv = vg[:, l0:l0 + LANES]
            for e in range(2):
                hl = 2 * pi + e
                h = gi * FP_GH + hl
                own = left if e == 0 else jnp.logical_not(left)
                fh = feat[:, hl * LANES:(hl + 1) * LANES]
                qa_ref[h] = jnp.where(own, qn, fh).astype(BF16)
                ka_ref[h] = jnp.where(own, kn, pltpu.roll(fh, LANES - F_SLOTS, 1)).astype(BF16)
                vt_ref[h, 0] = jnp.where(own, vv, 1.0).T.astype(BF16)

    n_groups = D_MODEL // FP_GW
    pending = group_products(0)
    for gi in range(n_groups):
        nxt = group_products(gi + 1) if gi + 1 < n_groups else None
        group_finish(gi, pending)
        pending = nxt


def _fox_prep_call(x, mod, w_in, b_f, gq, gk, place):
    bsz, seq, _ = x.shape
    small = [w_in, b_f, gq, gk, place]
    return _tiled_call(
        _fox_prep_kernel,
        grid=(bsz, seq // FP_T),
        in_specs=[pl.BlockSpec((None, FP_T, D_MODEL), lambda b, t: (b, t, 0)),
                  pl.BlockSpec((None, 1, 3 * D_MODEL), lambda b, t: (b, 0, 0))]
        + [_resident(a.shape) for a in small],
        out_specs=[
            pl.BlockSpec((None, FOX_HEADS, FP_T, LANES), lambda b, t: (b, 0, t, 0)),
            pl.BlockSpec((None, FOX_HEADS, FP_T, LANES), lambda b, t: (b, 0, t, 0)),
            pl.BlockSpec((None, FOX_HEADS, 1, LANES, FP_T), lambda b, t: (b, 0, t, 0, 0)),
            pl.BlockSpec((None, FP_T, D_MODEL), lambda b, t: (b, t, 0)),
        ],
        out_shape=[
            jax.ShapeDtypeStruct((bsz, FOX_HEADS, seq, LANES), BF16),
            jax.ShapeDtypeStruct((bsz, FOX_HEADS, seq, LANES), BF16),
            jax.ShapeDtypeStruct((bsz, FOX_HEADS, seq // FP_T, LANES, FP_T), BF16),
            jax.ShapeDtypeStruct((bsz, seq, D_MODEL), BF16),
        ],
        scratch_shapes=[pltpu.VMEM((SUBLANES, LANES), F32)],
        name="fox_prep",
    )(x, mod, *small)


FA_T = FP_T
FA_NH = 4
FA_NP = FA_NH // 2
FA_GROUP = 3
FA_ROW_PAD = LANES


def _fox_attn_kernel(qa_ref, ka_ref, vt_ref, g_ref, o_ref, m_ref, acc_ref, *s_refs):
    seq = qa_ref.shape[1]
    n_q = seq // FA_T
    n_items = n_q * (n_q + 1) // 2
    nt = (((1,), (1,)), ((), ()))
    key = lax.broadcasted_iota(jnp.int32, (FA_T, FA_T), 0)
    qry = lax.broadcasted_iota(jnp.int32, (FA_T, FA_T), 1)
    top = lax.broadcasted_iota(jnp.int32, (LANES, FA_T), 0) < FOX_HD

    def reset_stats():
        m_ref[...] = jnp.full_like(m_ref, MASK_NEG)
        acc_ref[...] = jnp.zeros_like(acc_ref)

    def scores(qi, j, dst_ref, h):
        q0 = pl.multiple_of(qi * FA_T, FA_T)
        k0 = pl.multiple_of(j * FA_T, FA_T)
        dst_ref[h, :, 0:FA_T] = lax.dot_general(ka_ref[h, pl.ds(k0, FA_T), :], qa_ref[h, pl.ds(q0, FA_T), :],
                                                nt, preferred_element_type=F32)

    def softmax_pv(j, src_ref, h, masked):
        s_t = src_ref[h, :, 0:FA_T]
        if masked:
            s_t = jnp.where(key <= qry, s_t, MASK_NEG)
        m_prev = m_ref[h]
        m_new = jnp.maximum(m_prev, jnp.max(s_t, axis=0, keepdims=True))
        p_t = jnp.exp2(s_t - m_new).astype(BF16)
        acc_ref[h, :, 0:FA_T] = jnp.exp2(m_prev - m_new) * acc_ref[h, :, 0:FA_T] + _bdot(vt_ref[h, j], p_t)
        m_ref[h] = m_new

    def finish_tile(qi):
        q0 = pl.multiple_of(qi * FA_T, FA_T)
        for p in range(FA_NP):
            acc0 = acc_ref[2 * p, :, 0:FA_T]
            acc1 = acc_ref[2 * p + 1, :, 0:FA_T]
            o_t = jnp.where(top, acc0 * (1.0 / acc0[FOX_HD:FOX_HD + 1, :]), acc1 * (1.0 / acc1[0:1, :]))
            g = g_ref[pl.ds(q0, FA_T), p * LANES:(p + 1) * LANES].astype(F32)
            o_ref[pl.ds(q0, FA_T), p * LANES:(p + 1) * LANES] = (o_t.T * g).astype(BF16)
        reset_stats()

    def advance(qi, j):
        diag = j == qi
        return diag, jnp.where(diag, jnp.minimum(qi + 1, n_q - 1), qi), jnp.where(diag, 0, j + 1)

    def item(qi, j, cur_ref, masked, nxt, nxt_ref):
        for h in range(FA_NH):
            scores(nxt[0], nxt[1], nxt_ref, h)
            softmax_pv(j, cur_ref, h, masked)
        if masked:
            finish_tile(qi)

    reset_stats()
    for h in range(FA_NH):
        scores(0, 0, s_refs[0], h)

    patterns = [p for p in itertools.product((False, True), repeat=FA_GROUP)
                if not any(p[k] and p[k + 1] for k in range(FA_GROUP - 1))]

    def group_body(i, carry):
        qs, js, diags = [carry[0]], [carry[1]], []
        for _ in range(FA_GROUP):
            diag, q_n, j_n = advance(qs[-1], js[-1])
            diags.append(diag)
            qs.append(q_n)
            js.append(j_n)

        def run(pattern):
            for k in range(FA_GROUP):
                item(qs[k], js[k], s_refs[k % len(s_refs)], pattern[k], (qs[k + 1], js[k + 1]),
                     s_refs[(k + 1) % len(s_refs)])

        for pattern in patterns:
            cond = functools.reduce(jnp.logical_and,
                                    [d if m else jnp.logical_not(d) for d, m in zip(diags, pattern)])
            pl.when(cond)(functools.partial(run, pattern))
        return qs[-1], js[-1]

    assert n_q >= 2 and n_items % FA_GROUP == 0 and FA_GROUP % len(s_refs) == 0
    lax.fori_loop(0, n_items // FA_GROUP, group_body, (jnp.int32(0), jnp.int32(0)))


def _fox_attn_call(qa, ka, vt, og):
    bsz, _, seq, _ = qa.shape
    gw = FA_NP * LANES
    return _tiled_call(
        _fox_attn_kernel,
        grid=(bsz, FOX_HEADS // FA_NH),
        in_specs=[
            pl.BlockSpec((None, FA_NH, seq, LANES), lambda b, p: (b, p, 0, 0)),
            pl.BlockSpec((None, FA_NH, seq, LANES), lambda b, p: (b, p, 0, 0)),
            pl.BlockSpec((None, FA_NH, seq // FA_T, LANES, FA_T), lambda b, p: (b, p, 0, 0, 0)),
            pl.BlockSpec((None, seq, gw), lambda b, p: (b, 0, p)),
        ],
        out_specs=pl.BlockSpec((None, seq, gw), lambda b, p: (b, 0, p)),
        out_shape=jax.ShapeDtypeStruct((bsz, seq, D_MODEL), BF16),
        scratch_shapes=[
            pltpu.VMEM((FA_NH, 1, FA_T), F32),
            pltpu.VMEM((FA_NH, LANES, FA_T + FA_ROW_PAD), F32),
        ] + [pltpu.VMEM((FA_NH, FA_T, FA_T + FA_ROW_PAD), F32)] * FA_GROUP,
        name="fox_attn",
    )(qa, ka, vt, og)


def _pad_cols(w, n):
    return jnp.pad(w, ((0, 0), (0, n - w.shape[1])))


def _gate_weights(w_a, w_x):
    per_half = RG_BLOCKS // 2
    half = RG_WIDTH // 2

    def bd(w, j):
        m = jnp.zeros((half, half), F32)
        for i in range(per_half):
            m = lax.dynamic_update_slice(m, w[j * per_half + i], (i * RG_BLOCK, i * RG_BLOCK))
        return m

    return jnp.stack([jnp.concatenate([bd(w_a, j), bd(w_x, j)], axis=1) for j in range(2)]).astype(BF16)


def kernel(x, c, ada_w, ada_b, ln_g, ln_b, ev_w_in, ev_conv_w, ev_conv_b, ev_rg_wa, ev_rg_ba, ev_rg_wx,
           ev_rg_bx, ev_rg_lam, ev_gla_w_up, ev_gla_b_up, ev_gla_norm_g, ev_w_out, od_w_in, od_b_f,
           od_q_norm_g, od_k_norm_g, od_w_out, mlp_w1, mlp_b1, mlp_w2, mlp_b2):
    bsz = x.shape[0]
    mods = _ada_call(c, ada_w.reshape(2 * DEPTH, D_MODEL, 3 * D_MODEL),
                     ada_b.reshape(2 * DEPTH, 1, 3 * D_MODEL))
    mods = mods.reshape(2 * DEPTH, bsz, 1, 3 * D_MODEL)
    place = jnp.asarray(_fox_place_matrix(), BF16)

    w1_all = mlp_w1.astype(BF16)
    w2_all = mlp_w2.astype(BF16)
    for layer in range(DEPTH):
        mod_mix = mods[2 * layer]
        mod_mlp = mods[2 * layer + 1]
        mlp_params = [w1_all[layer], mlp_b1[layer][None], w2_all[layer],
                      mlp_b2[layer][None], ln_g[layer, 1][None], ln_b[layer, 1][None]]
        if layer % 2 == 0:
            e = layer // 2
            w_up = jnp.pad(ev_gla_w_up[e], ((0, LANES - GLA_LOWRANK), (0, 0)))
            mixer_params = [
                _pad_cols(ev_w_in[e].astype(BF16), EVEN_IN_PAD), ev_conv_w[e], ev_conv_b[e][None],
                _gate_weights(ev_rg_wa[e], ev_rg_wx[e]), ev_rg_ba[e][None], ev_rg_bx[e][None],
                ev_rg_lam[e][None], w_up, ev_gla_b_up[e][None], ev_gla_norm_g[e][None],
                ev_w_out[e].astype(BF16), ln_g[layer, 0][None], ln_b[layer, 0][None]]
            x = _even_layer_call(x, mod_mix, mixer_params, mod_mlp, mlp_params)
        else:
            o = layer // 2
            w_in = _pad_cols(od_w_in[o].astype(BF16), ODD_IN_PAD)
            b_f = jnp.pad(od_b_f[o], (0, LANES - FOX_HEADS))[None]
            gq = jnp.tile(od_q_norm_g[o], 2)[None]
            gk = jnp.tile(od_k_norm_g[o], 2)[None]
            qa, ka, vt, og = _fox_prep_call(x, mod_mix, w_in, b_f, gq, gk, place)
            mix = _fox_attn_call(qa, ka, vt, og)
            x = _post_call(mix, od_w_out[o].astype(BF16), x, mod_mix, ln_g[layer, 0][None],
                           ln_b[layer, 0][None], mod_mlp, mlp_params)
    return x
```

```python
import functools
import itertools

import numpy as np
import jax
import jax.numpy as jnp
from jax import lax
from jax.experimental import pallas as pl
from jax.experimental.pallas import tpu as pltpu

F32 = jnp.float32
BF16 = jnp.bfloat16

D_MODEL = 1024
DEPTH = 2
RG_WIDTH = D_MODEL // 2
RG_BLOCKS = 8
RG_BLOCK = RG_WIDTH // RG_BLOCKS
CONV_WIDTH = 4
RG_C = 8.0
GLA_HEADS = 4
GLA_VDIM = D_MODEL // 2
GLA_KDIM = GLA_VDIM // 2
GLA_HK = GLA_KDIM // GLA_HEADS
GLA_HV = GLA_VDIM // GLA_HEADS
GLA_LOWRANK = 16
GLA_GATE_TAU = 16.0
GLA_CHUNK = 64
EVEN_IN = 2 * RG_WIDTH + 2 * GLA_KDIM + 2 * GLA_VDIM + GLA_LOWRANK
FOX_HEADS = 16
FOX_HD = D_MODEL // FOX_HEADS
FOX_PAIRS = FOX_HEADS // 2
ODD_IN = 4 * D_MODEL + FOX_HEADS
D_FF = 4 * D_MODEL
ALPHA = (2 * DEPTH) ** 0.25
LN_EPS = 1e-5
RMS_EPS = 1e-6

LANES = 128
SUBLANES = 8
VMEM_LIMIT = 56 * 1024 * 1024

EVEN_IN_PAD = EVEN_IN + (LANES - GLA_LOWRANK)
ODD_IN_PAD = ODD_IN + (LANES - FOX_HEADS)
MASK_NEG = -1e30
LOG2E = float(np.log2(np.e))

F_PARTS = 3
F_SLOTS = 8

PROJ_TN = 512


def _cparams(sem):
    return pltpu.CompilerParams(dimension_semantics=sem, vmem_limit_bytes=VMEM_LIMIT)


def _resident(shape):
    nd = len(shape)
    return pl.BlockSpec(shape, lambda *_: (0,) * nd, pipeline_mode=pl.Buffered(1))


def _sigmoid(x):
    return 0.5 * jnp.tanh(0.5 * x) + 0.5


def _log_sigmoid(x):
    return jnp.minimum(x, 0.0) - jnp.log1p(jnp.exp(-jnp.abs(x)))


def _gelu_tanh(x):
    c = np.sqrt(2.0 / np.pi).astype(np.float32)
    return x * (0.5 * (1.0 + jnp.tanh(c * (x + 0.044715 * (x * x * x)))))


def _sqrt_nonneg(y):
    return jnp.where(y > 0.0, y * lax.rsqrt(y), 0.0)


def _bdot(a, b):
    return jnp.dot(a, b, preferred_element_type=F32)


def _split2(a):
    hi = a.astype(BF16)
    lo = (a - hi.astype(F32)).astype(BF16)
    return hi, lo


def _dot_split(a, w):
    ah, al = _split2(a)
    wh, wl = _split2(w)
    return _bdot(ah, wh) + (_bdot(ah, wl) + _bdot(al, wh))


def _layer_norm(z, g, b):
    mu = jnp.mean(z, axis=-1, keepdims=True)
    zc = z - mu
    var = jnp.mean(zc * zc, axis=-1, keepdims=True)
    return zc * lax.rsqrt(var + LN_EPS) * g + b


LN_ROWS = 64


def _ln_rows(src_ref, g_ref, b_ref, emit):
    g = g_ref[...]
    b = b_ref[...]
    for r0 in range(0, src_ref.shape[0], LN_ROWS):
        rows = slice(r0, r0 + LN_ROWS)
        emit(rows, _layer_norm(src_ref[rows, :], g, b))


def _modulate(x, mod):
    shift = mod[:, 0:D_MODEL]
    scale = mod[:, D_MODEL:2 * D_MODEL]
    return x * (1.0 + scale) + shift


def _gate(mod):
    return 1.0 + mod[:, 2 * D_MODEL:3 * D_MODEL]


MLP_UP = 512
MLP_DOWN = 256


def _mlp_pieces(src_ref, g_ref, b_ref, mod, w1_ref, b1_ref, w2_ref, b2_ref, xn_ref, u_ref, hid_ref, z_ref):
    gate = _gate(mod)

    def prepare():
        def emit(rows, xn):
            xn_ref[rows, :] = xn
            u_ref[rows, :] = _modulate(xn, mod).astype(BF16)
        _ln_rows(src_ref, g_ref, b_ref, emit)

    def up(f):
        f0 = f * MLP_UP
        hdn = _bdot(u_ref[...], w1_ref[:, f0:f0 + MLP_UP]) + b1_ref[:, f0:f0 + MLP_UP]
        hdn = jnp.maximum(hdn, 0.0)
        hid_ref[:, f0:f0 + MLP_UP] = (hdn * hdn).astype(BF16)

    def down(n):
        n0 = n * MLP_DOWN
        y = _bdot(hid_ref[...], w2_ref[:, n0:n0 + MLP_DOWN]) + b2_ref[:, n0:n0 + MLP_DOWN]
        z_ref[:, n0:n0 + MLP_DOWN] = ALPHA * xn_ref[:, n0:n0 + MLP_DOWN] + gate[:, n0:n0 + MLP_DOWN] * y

    return prepare, up, down


ADA_TN = 1024


def _ada_kernel(c_ref, w_ref, b_ref, o_ref):
    c = c_ref[...]
    s = c * _sigmoid(c)
    o_ref[...] = _dot_split(s, w_ref[...]) + b_ref[...]


def _ada_call(c, ada_w, ada_b):
    n_mod = ada_w.shape[0]
    bsz = c.shape[0]
    return pl.pallas_call(
        _ada_kernel,
        grid=(n_mod, 3 * D_MODEL // ADA_TN),
        in_specs=[
            pl.BlockSpec((bsz, D_MODEL), lambda i, j: (0, 0)),
            pl.BlockSpec((None, D_MODEL, ADA_TN), lambda i, j: (i, 0, j)),
            pl.BlockSpec((None, 1, ADA_TN), lambda i, j: (i, 0, j)),
        ],
        out_specs=pl.BlockSpec((None, bsz, ADA_TN), lambda i, j: (i, 0, j)),
        out_shape=jax.ShapeDtypeStruct((n_mod, bsz, 3 * D_MODEL), F32),
        compiler_params=_cparams(("arbitrary", "arbitrary")),
        name="ada",
    )(c, ada_w, ada_b)


EV_T = 512
EV_CHUNKS = EV_T // GLA_CHUNK
_XR, _YR = 0, RG_WIDTH
_Q = 2 * RG_WIDTH
_K = _Q + GLA_KDIM
_V = _K + GLA_KDIM
_G = _V + GLA_VDIM
_ZL = _G + GLA_VDIM


def _scan_rows8(a, u):
    rows, ch = a.shape
    a3 = a.reshape(rows // SUBLANES, SUBLANES, ch)
    u3 = u.reshape(rows // SUBLANES, SUBLANES, ch)
    sub = lax.broadcasted_iota(jnp.int32, a3.shape, 1)
    d = 1
    while d < SUBLANES:
        keep = sub >= d
        u3 = jnp.where(keep, a3 * pltpu.roll(u3, d, 1) + u3, u3)
        a3 = jnp.where(keep, a3 * pltpu.roll(a3, d, 1), a3)
        d *= 2
    return a3, u3


def _even_layer_kernel(x_ref, mod_ref, win_ref, convw_ref, convb_ref, wg_ref, ba_ref, bx_ref, lam_ref,
                       wup_ref, bup_ref, gn_ref, wout_ref, lng_ref, lnb_ref,
                       mod2_ref, w1_ref, b1_ref, w2_ref, b2_ref, lng2_ref, lnb2_ref, o_ref,
                       xmid_ref, xn_ref, umlp_ref, hid_ref, zmlp_ref, p_ref, mix_ref, tail_ref, hc_ref, st_ref,
                       qd_ref, kd_ref, ke_ref, v_ref, bc_ref, oo_ref, *, tiles_per_seq, n_tiles):
    step = pl.program_id(0)
    t_idx = jnp.minimum(step, n_tiles - 1) % tiles_per_seq

    @pl.when(step == 0)
    def _():
        xmid_ref[...] = jnp.zeros_like(xmid_ref)

    @pl.when(t_idx == 0)
    def _():
        tail_ref[...] = jnp.zeros_like(tail_ref)
        hc_ref[...] = jnp.zeros_like(hc_ref)
        st_ref[...] = jnp.zeros_like(st_ref)

    mlp_prepare, mlp_up, mlp_down = _mlp_pieces(xmid_ref, lng_ref, lnb_ref, mod2_ref[...], w1_ref, b1_ref,
                                                w2_ref, b2_ref, xn_ref, umlp_ref, hid_ref, zmlp_ref)
    mlp_prepare()

    u_in = _modulate(x_ref[...], mod_ref[...]).astype(BF16)
    for c0 in range(0, EVEN_IN_PAD, PROJ_TN):
        c1 = min(c0 + PROJ_TN, EVEN_IN_PAD)
        p_ref[:, c0:c1] = _bdot(u_in, win_ref[:, c0:c1])
    mlp_up(0)

    xr = p_ref[:, _XR:_XR + RG_WIDTH]
    tail = tail_ref[...]
    row8 = lax.broadcasted_iota(jnp.int32, (SUBLANES, RG_WIDTH), 0)
    xc = convb_ref[...] + convw_ref[CONV_WIDTH - 1:CONV_WIDTH, :] * xr
    for j in range(1, CONV_WIDTH):
        xs = pltpu.roll(xr, j, 0)
        head = jnp.where(row8 < j, pltpu.roll(tail, j, 0), xs[0:SUBLANES])
        xs = jnp.concatenate([head, xs[SUBLANES:]], axis=0)
        xc = xc + convw_ref[CONV_WIDTH - 1 - j:CONV_WIDTH - j, :] * xs
    tail_ref[...] = xr[EV_T - SUBLANES:EV_T]

    half = RG_WIDTH // 2
    r_parts, i_parts = [], []
    for j in range(2):
        ri = _bdot(xc[:, j * half:(j + 1) * half].astype(BF16), wg_ref[j])
        r_parts.append(ri[:, :half])
        i_parts.append(ri[:, half:])
    for f in range(1, 3):
        mlp_up(f)
    r = _sigmoid(jnp.concatenate(r_parts, axis=1) + ba_ref[...])
    ig = _sigmoid(jnp.concatenate(i_parts, axis=1) + bx_ref[...])
    log_a = RG_C * r * _log_sigmoid(lam_ref[...])
    a = jnp.exp(log_a)
    u = _sqrt_nonneg(-jnp.tanh(log_a) * (a * a + 1.0)) * (ig * xc)

    a3, h3 = _scan_rows8(a, u)
    carry = jnp.broadcast_to(hc_ref[SUBLANES - 1:SUBLANES, :], (SUBLANES, RG_WIDTH))
    h_groups = []
    n_groups = EV_T // SUBLANES
    for g in range(n_groups):
        hg = h3[g] + a3[g] * carry
        h_groups.append(hg)
        carry = jnp.broadcast_to(hg[SUBLANES - 1:SUBLANES, :], (SUBLANES, RG_WIDTH))
        if g == n_groups // 3:
            mlp_up(3)
        if g == 2 * n_groups // 3:
            mlp_up(4)
    hc_ref[...] = carry
    h = jnp.concatenate(h_groups, axis=0)
    mix_ref[:, 0:RG_WIDTH] = (h * _gelu_tanh(p_ref[:, _YR:_YR + RG_WIDTH])).astype(BF16)
    mlp_up(5)

    z = _dot_split(p_ref[:, _ZL:_ZL + LANES], wup_ref[...]) + bup_ref[...]
    bc = _log_sigmoid(z) * (1.0 / GLA_GATE_TAU)
    mlp_up(6)
    rowk = lax.broadcasted_iota(jnp.int32, (EV_T, GLA_KDIM), 0) & (GLA_CHUNK - 1)
    d = 1
    while d < GLA_CHUNK:
        bc = jnp.where(rowk >= d, bc + pltpu.roll(bc, d, 0), bc)
        d *= 2
    bc3 = bc.reshape(EV_CHUNKS, GLA_CHUNK, GLA_KDIM)
    b_last = jnp.broadcast_to(bc3[:, GLA_CHUNK - 1:GLA_CHUNK, :], bc3.shape).reshape(EV_T, GLA_KDIM)
    q = p_ref[:, _Q:_Q + GLA_KDIM]
    k = p_ref[:, _K:_K + GLA_KDIM]
    qd_ref[...] = (q * (GLA_HK ** -0.5) * jnp.exp(bc)).astype(BF16)
    kd_ref[...] = (k * jnp.exp(-bc)).astype(BF16)
    ke_ref[...] = (k * jnp.exp(b_last - bc)).astype(BF16)
    v_ref[...] = p_ref[:, _V:_V + GLA_VDIM].astype(BF16)
    bc_ref[...] = bc

    lane_k = lax.broadcasted_iota(jnp.int32, (GLA_CHUNK, GLA_KDIM), 1) // GLA_HK
    lane_v = lax.broadcasted_iota(jnp.int32, (GLA_CHUNK, GLA_VDIM), 1) // GLA_HV
    tri_r = lax.broadcasted_iota(jnp.int32, (GLA_CHUNK, GLA_HEADS * GLA_CHUNK), 0)
    tri_c = lax.broadcasted_iota(jnp.int32, (GLA_CHUNK, GLA_HEADS * GLA_CHUNK), 1) & (GLA_CHUNK - 1)
    st_r = lax.broadcasted_iota(jnp.int32, (GLA_VDIM, GLA_KDIM), 0) // GLA_HV
    st_c = lax.broadcasted_iota(jnp.int32, (GLA_VDIM, GLA_KDIM), 1) // GLA_HK
    nt = (((1,), (1,)), ((), ()))
    tn = (((0,), (0,)), ((), ()))

    def chunk_body(c):
        r0 = c * GLA_CHUNK
        qd = qd_ref[r0:r0 + GLA_CHUNK, :]
        kd = kd_ref[r0:r0 + GLA_CHUNK, :]
        ke = ke_ref[r0:r0 + GLA_CHUNK, :]
        vv = v_ref[r0:r0 + GLA_CHUNK, :]
        zk = jnp.zeros_like(kd)
        zv = jnp.zeros_like(vv)
        kbd = jnp.concatenate([jnp.where(lane_k == hh, kd, zk) for hh in range(GLA_HEADS)], axis=0)
        att = lax.dot_general(qd, kbd, nt, preferred_element_type=F32)
        att = jnp.where(tri_c <= tri_r, att, 0.0).astype(BF16)
        vbd = jnp.concatenate([jnp.where(lane_v == hh, vv, zv) for hh in range(GLA_HEADS)], axis=0)
        st = st_ref[...]
        o = _bdot(att, vbd) + lax.dot_general(qd, st.astype(BF16), nt, preferred_element_type=F32)
        oo_ref[r0:r0 + GLA_CHUNK, :] = o
        kv_t = lax.dot_general(vv, ke, tn, preferred_element_type=F32)
        decay = jnp.exp(bc_ref[r0 + GLA_CHUNK - 1:r0 + GLA_CHUNK, :])
        st_ref[...] = decay * st + jnp.where(st_r == st_c, kv_t, 0.0)

    mlp_up(7)
    for c in range(EV_CHUNKS):
        chunk_body(c)
        if c % 2 == 1:
            mlp_down(c // 2)

    for hh in range(GLA_HEADS):
        oh = oo_ref[:, hh * GLA_HV:(hh + 1) * GLA_HV]
        ms = jnp.mean(oh * oh, axis=-1, keepdims=True)
        gg = p_ref[:, _G + hh * GLA_HV:_G + (hh + 1) * GLA_HV]
        on = oh * lax.rsqrt(ms + RMS_EPS) * gn_ref[:, hh * GLA_HV:(hh + 1) * GLA_HV]
        mix_ref[:, RG_WIDTH + hh * GLA_HV:RG_WIDTH + (hh + 1) * GLA_HV] = (on * (gg * _sigmoid(gg))).astype(BF16)

    y = _bdot(mix_ref[...], wout_ref[...])

    def emit_out(rows, xn):
        o_ref[rows, :] = xn
    _ln_rows(zmlp_ref, lng2_ref, lnb2_ref, emit_out)
    xmid_ref[...] = ALPHA * x_ref[...] + _gate(mod_ref[...]) * y


def _even_layer_call(x, mod, mixer_params, mod_mlp, mlp_params):
    bsz, seq, _ = x.shape
    tiles_per_seq = seq // EV_T
    n_tiles = bsz * tiles_per_seq

    def mix_tile(s):
        return jnp.minimum(s, n_tiles - 1)

    def mlp_tile(s):
        return jnp.maximum(s - 1, 0)

    def tile_spec(tile_of):
        return pl.BlockSpec((None, EV_T, D_MODEL),
                            lambda s: (tile_of(s) // tiles_per_seq, tile_of(s) % tiles_per_seq, 0))

    def mod_spec(tile_of):
        return pl.BlockSpec((None, 1, 3 * D_MODEL), lambda s: (tile_of(s) // tiles_per_seq, 0, 0))

    return pl.pallas_call(
        functools.partial(_even_layer_kernel, tiles_per_seq=tiles_per_seq, n_tiles=n_tiles),
        grid=(n_tiles + 1,),
        in_specs=[tile_spec(mix_tile), mod_spec(mix_tile)]
        + [_resident(a.shape) for a in mixer_params] + [mod_spec(mlp_tile)]
        + [_resident(a.shape) for a in mlp_params],
        out_specs=tile_spec(mlp_tile),
        out_shape=jax.ShapeDtypeStruct((bsz, seq, D_MODEL), F32),
        scratch_shapes=[
            pltpu.VMEM((EV_T, D_MODEL), F32),
            pltpu.VMEM((EV_T, D_MODEL), F32),
            pltpu.VMEM((EV_T, D_MODEL), BF16),
            pltpu.VMEM((EV_T, D_FF), BF16),
            pltpu.VMEM((EV_T, D_MODEL), F32),
            pltpu.VMEM((EV_T, EVEN_IN_PAD), F32),
            pltpu.VMEM((EV_T, D_MODEL), BF16),
            pltpu.VMEM((SUBLANES, RG_WIDTH), F32),
            pltpu.VMEM((SUBLANES, RG_WIDTH), F32),
            pltpu.VMEM((GLA_VDIM, GLA_KDIM), F32),
            pltpu.VMEM((EV_T, GLA_KDIM), BF16),
            pltpu.VMEM((EV_T, GLA_KDIM), BF16),
            pltpu.VMEM((EV_T, GLA_KDIM), BF16),
            pltpu.VMEM((EV_T, GLA_VDIM), BF16),
            pltpu.VMEM((EV_T, GLA_KDIM), F32),
            pltpu.VMEM((EV_T, GLA_VDIM), F32),
        ],
        compiler_params=_cparams(("arbitrary",)),
        name="even_layer",
    )(x, mod, *mixer_params, mod_mlp, *mlp_params)


POST_TM = 512


def _post_kernel(a_ref, wout_ref, x_ref, mod_ref, lng_ref, lnb_ref,
                 mod2_ref, w1_ref, b1_ref, w2_ref, b2_ref, lng2_ref, lnb2_ref, o_ref,
                 zmix_ref, xn_ref, umlp_ref, hid_ref, zmlp_ref):
    y = _bdot(a_ref[...], wout_ref[...])
    zmix_ref[...] = ALPHA * x_ref[...] + _gate(mod_ref[...]) * y
    mlp_prepare, mlp_up, mlp_down = _mlp_pieces(zmix_ref, lng_ref, lnb_ref, mod2_ref[...], w1_ref, b1_ref,
                                                w2_ref, b2_ref, xn_ref, umlp_ref, hid_ref, zmlp_ref)
    mlp_prepare()
    for f in range(D_FF // MLP_UP):
        mlp_up(f)
    for n in range(D_MODEL // MLP_DOWN):
        mlp_down(n)

    def emit_out(rows, xn):
        o_ref[rows, :] = xn
    _ln_rows(zmlp_ref, lng2_ref, lnb2_ref, emit_out)


def _post_call(a, w_out, x, mod, ln_g, ln_b, mod_mlp, mlp_params):
    bsz, seq, _ = x.shape
    tile = pl.BlockSpec((None, POST_TM, D_MODEL), lambda b, t: (b, t, 0))
    mod_spec = pl.BlockSpec((None, 1, 3 * D_MODEL), lambda b, t: (b, 0, 0))
    return pl.pallas_call(
        _post_kernel,
        grid=(bsz, seq // POST_TM),
        in_specs=[tile, _resident(w_out.shape), tile, mod_spec, _resident(ln_g.shape), _resident(ln_b.shape),
                  mod_spec] + [_resident(p.shape) for p in mlp_params],
        out_specs=tile,
        out_shape=jax.ShapeDtypeStruct((bsz, seq, D_MODEL), F32),
        scratch_shapes=[
            pltpu.VMEM((POST_TM, D_MODEL), F32),
            pltpu.VMEM((POST_TM, D_MODEL), F32),
            pltpu.VMEM((POST_TM, D_MODEL), BF16),
            pltpu.VMEM((POST_TM, D_FF), BF16),
            pltpu.VMEM((POST_TM, D_MODEL), F32),
        ],
        compiler_params=_cparams(("arbitrary", "arbitrary")),
        name="post",
    )(a, w_out, x, mod, ln_g, ln_b, mod_mlp, *mlp_params)


FP_T = 512
FP_GW = 256
FP_GH = FP_GW // FOX_HD
_OQ, _OK, _OV, _OG, _OF = 0, D_MODEL, 2 * D_MODEL, 3 * D_MODEL, 4 * D_MODEL
F_ONE_COL = F_PARTS * FOX_HEADS


def _fox_place_matrix():
    place = np.zeros((LANES, FOX_HEADS * LANES), np.float32)
    for h in range(FOX_HEADS):
        base = h * LANES + (FOX_HD if h % 2 == 0 else 0)
        for part in range(F_PARTS):
            place[part * FOX_HEADS + h, base + part] = 1.0
            place[F_ONE_COL, base + F_PARTS + part] = 1.0
            place[F_ONE_COL, base + F_SLOTS + part] = 1.0
            place[part * FOX_HEADS + h, base + F_SLOTS + F_PARTS + part] = -1.0
    return place


def _fox_prep_kernel(x_ref, mod_ref, win_ref, bf_ref, gq_ref, gk_ref, place_ref,
                     qa_ref, ka_ref, vt_ref, og_ref, fc_ref):
    t_idx = pl.program_id(1)

    @pl.when(t_idx == 0)
    def _():
        fc_ref[...] = jnp.zeros_like(fc_ref)

    u_in = _modulate(x_ref[...], mod_ref[...]).astype(BF16)

    def proj(c0, width):
        return _bdot(u_in, win_ref[:, c0:c0 + width])

    lane = lax.broadcasted_iota(jnp.int32, (FP_T, LANES), 1)
    f = _log_sigmoid(proj(_OF, LANES) + bf_ref[...])
    row = lax.broadcasted_iota(jnp.int32, (FP_T, LANES), 0)
    d = 1
    while d < FP_T:
        f = jnp.where(row >= d, f + pltpu.roll(f, d, 0), f)
        d *= 2
    f = f + fc_ref[SUBLANES - 1:SUBLANES, :]
    fc_ref[...] = f[FP_T - SUBLANES:FP_T]
    f = jnp.where(lane < FOX_HEADS, f * LOG2E, 0.0)
    f1 = f.astype(BF16).astype(F32)
    r1 = f - f1
    f2 = r1.astype(BF16).astype(F32)
    f3 = (r1 - f2).astype(BF16).astype(F32)
    packed = f1 + pltpu.roll(f2, FOX_HEADS, 1) + pltpu.roll(f3, 2 * FOX_HEADS, 1)
    packed = jnp.where(lane == F_ONE_COL, 1.0, packed).astype(BF16)

    left = lane < FOX_HD

    def head_norm(xp, g):
        sq = xp * xp
        ms_l = jnp.sum(jnp.where(left, sq, 0.0), axis=-1, keepdims=True)
        ms_r = jnp.sum(jnp.where(left, 0.0, sq), axis=-1, keepdims=True)
        ms = jnp.where(left, ms_l, ms_r) * (1.0 / FOX_HD)
        return xp * lax.rsqrt(ms + RMS_EPS) * g

    def group_products(gi):
        c0 = gi * FP_GW
        h0 = c0 // FOX_HD
        return (proj(_OQ + c0, FP_GW), proj(_OK + c0, FP_GW), proj(_OV + c0, FP_GW), proj(_OG + c0, FP_GW),
                _bdot(packed, place_ref[:, h0 * LANES:(h0 + FP_GH) * LANES]))

    def group_finish(gi, prods):
        qg, kg, vg, gg, feat = prods
        c0 = gi * FP_GW
        og_ref[:, c0:c0 + FP_GW] = _sigmoid(gg).astype(BF16)
        for pi in range(FP_GW // LANES):
            l0 = pi * LANES
            qn = head_norm(qg[:, l0:l0 + LANES], gq_ref[...]) * (FOX_HD ** -0.5 * LOG2E)
            kn = head_norm(kg[:, l0:l0 + LANES], gk_ref[...])
            vv = vg[:, l0:l0 + LANES]
            for e in range(2):
                hl = 2 * pi + e
                h = gi * FP_GH + hl
                own = left if e == 0 else jnp.logical_not(left)
                fh = feat[:, hl * LANES:(hl + 1) * LANES]
                qa_ref[h] = jnp.where(own, qn, fh).astype(BF16)
                ka_ref[h] = jnp.where(own, kn, pltpu.roll(fh, LANES - F_SLOTS, 1)).astype(BF16)
                vt_ref[h, 0] = jnp.where(own, vv, 1.0).T.astype(BF16)

    n_groups = D_MODEL // FP_GW
    pending = group_products(0)
    for gi in range(n_groups):
        nxt = group_products(gi + 1) if gi + 1 < n_groups else None
        group_finish(gi, pending)
        pending = nxt


def _fox_prep_call(x, mod, w_in, b_f, gq, gk, place):
    bsz, seq, _ = x.shape
    small = [w_in, b_f, gq, gk, place]
    return pl.pallas_call(
        _fox_prep_kernel,
        grid=(bsz, seq // FP_T),
        in_specs=[pl.BlockSpec((None, FP_T, D_MODEL), lambda b, t: (b, t, 0)),
                  pl.BlockSpec((None, 1, 3 * D_MODEL), lambda b, t: (b, 0, 0))]
        + [_resident(a.shape) for a in small],
        out_specs=[
            pl.BlockSpec((None, FOX_HEADS, FP_T, LANES), lambda b, t: (b, 0, t, 0)),
            pl.BlockSpec((None, FOX_HEADS, FP_T, LANES), lambda b, t: (b, 0, t, 0)),
            pl.BlockSpec((None, FOX_HEADS, 1, LANES, FP_T), lambda b, t: (b, 0, t, 0, 0)),
            pl.BlockSpec((None, FP_T, D_MODEL), lambda b, t: (b, t, 0)),
        ],
        out_shape=[
            jax.ShapeDtypeStruct((bsz, FOX_HEADS, seq, LANES), BF16),
            jax.ShapeDtypeStruct((bsz, FOX_HEADS, seq, LANES), BF16),
            jax.ShapeDtypeStruct((bsz, FOX_HEADS, seq // FP_T, LANES, FP_T), BF16),
            jax.ShapeDtypeStruct((bsz, seq, D_MODEL), BF16),
        ],
        scratch_shapes=[pltpu.VMEM((SUBLANES, LANES), F32)],
        compiler_params=_cparams(("arbitrary", "arbitrary")),
        name="fox_prep",
    )(x, mod, *small)


FA_T = FP_T
FA_NH = 4
FA_NP = FA_NH // 2
FA_GROUP = 3


def _fox_attn_kernel(qa_ref, ka_ref, vt_ref, g_ref, o_ref, m_ref, acc_ref, *s_refs):
    seq = qa_ref.shape[1]
    n_q = seq // FA_T
    n_items = n_q * (n_q + 1) // 2
    nt = (((1,), (1,)), ((), ()))
    key = lax.broadcasted_iota(jnp.int32, (FA_T, FA_T), 0)
    qry = lax.broadcasted_iota(jnp.int32, (FA_T, FA_T), 1)
    top = lax.broadcasted_iota(jnp.int32, (LANES, FA_T), 0) < FOX_HD

    def reset_stats():
        m_ref[...] = jnp.full_like(m_ref, MASK_NEG)
        acc_ref[...] = jnp.zeros_like(acc_ref)

    def scores(qi, j, dst_ref, h):
        q0 = pl.multiple_of(qi * FA_T, FA_T)
        k0 = pl.multiple_of(j * FA_T, FA_T)
        dst_ref[h] = lax.dot_general(ka_ref[h, pl.ds(k0, FA_T), :], qa_ref[h, pl.ds(q0, FA_T), :], nt,
                                     preferred_element_type=F32)

    def softmax_pv(j, src_ref, h, masked):
        s_t = src_ref[h]
        if masked:
            s_t = jnp.where(key <= qry, s_t, MASK_NEG)
        m_prev = m_ref[h]
        m_new = jnp.maximum(m_prev, jnp.max(s_t, axis=0, keepdims=True))
        p_t = jnp.exp2(s_t - m_new).astype(BF16)
        acc_ref[h] = jnp.exp2(m_prev - m_new) * acc_ref[h] + _bdot(vt_ref[h, j], p_t)
        m_ref[h] = m_new

    def finish_tile(qi):
        q0 = pl.multiple_of(qi * FA_T, FA_T)
        for p in range(FA_NP):
            acc0 = acc_ref[2 * p]
            acc1 = acc_ref[2 * p + 1]
            o_t = jnp.where(top, acc0 * (1.0 / acc0[FOX_HD:FOX_HD + 1, :]), acc1 * (1.0 / acc1[0:1, :]))
            g = g_ref[pl.ds(q0, FA_T), p * LANES:(p + 1) * LANES].astype(F32)
            o_ref[pl.ds(q0, FA_T), p * LANES:(p + 1) * LANES] = (o_t.T * g).astype(BF16)
        reset_stats()

    def advance(qi, j):
        diag = j == qi
        return diag, jnp.where(diag, jnp.minimum(qi + 1, n_q - 1), qi), jnp.where(diag, 0, j + 1)

    def item(qi, j, cur_ref, masked, nxt, nxt_ref):
        for h in range(FA_NH):
            scores(nxt[0], nxt[1], nxt_ref, h)
            softmax_pv(j, cur_ref, h, masked)
        if masked:
            finish_tile(qi)

    reset_stats()
    for h in range(FA_NH):
        scores(0, 0, s_refs[0], h)

    patterns = [p for p in itertools.product((False, True), repeat=FA_GROUP)
                if not any(p[k] and p[k + 1] for k in range(FA_GROUP - 1))]

    def group_body(i, carry):
        qs, js, diags = [carry[0]], [carry[1]], []
        for _ in range(FA_GROUP):
            diag, q_n, j_n = advance(qs[-1], js[-1])
            diags.append(diag)
            qs.append(q_n)
            js.append(j_n)

        def run(pattern):
            for k in range(FA_GROUP):
                item(qs[k], js[k], s_refs[k % len(s_refs)], pattern[k], (qs[k + 1], js[k + 1]),
                     s_refs[(k + 1) % len(s_refs)])

        for pattern in patterns:
            cond = functools.reduce(jnp.logical_and,
                                    [d if m else jnp.logical_not(d) for d, m in zip(diags, pattern)])
            pl.when(cond)(functools.partial(run, pattern))
        return qs[-1], js[-1]

    assert n_q >= 2 and n_items % FA_GROUP == 0 and FA_GROUP % len(s_refs) == 0
    lax.fori_loop(0, n_items // FA_GROUP, group_body, (jnp.int32(0), jnp.int32(0)))


def _fox_attn_call(qa, ka, vt, og):
    bsz, _, seq, _ = qa.shape
    gw = FA_NP * LANES
    return pl.pallas_call(
        _fox_attn_kernel,
        grid=(bsz, FOX_HEADS // FA_NH),
        in_specs=[
            pl.BlockSpec((None, FA_NH, seq, LANES), lambda b, p: (b, p, 0, 0)),
            pl.BlockSpec((None, FA_NH, seq, LANES), lambda b, p: (b, p, 0, 0)),
            pl.BlockSpec((None, FA_NH, seq // FA_T, LANES, FA_T), lambda b, p: (b, p, 0, 0, 0)),
            pl.BlockSpec((None, seq, gw), lambda b, p: (b, 0, p)),
        ],
        out_specs=pl.BlockSpec((None, seq, gw), lambda b, p: (b, 0, p)),
        out_shape=jax.ShapeDtypeStruct((bsz, seq, D_MODEL), BF16),
        scratch_shapes=[
            pltpu.VMEM((FA_NH, 1, FA_T), F32),
            pltpu.VMEM((FA_NH, LANES, FA_T), F32),
        ] + [pltpu.VMEM((FA_NH, FA_T, FA_T), F32)] * FA_GROUP,
        compiler_params=_cparams(("arbitrary", "arbitrary")),
        name="fox_attn",
    )(qa, ka, vt, og)


def _pad_cols(w, n):
    return jnp.pad(w, ((0, 0), (0, n - w.shape[1])))


def _gate_weights(w_a, w_x):
    per_half = RG_BLOCKS // 2
    half = RG_WIDTH // 2

    def bd(w, j):
        m = jnp.zeros((half, half), F32)
        for i in range(per_half):
            m = lax.dynamic_update_slice(m, w[j * per_half + i], (i * RG_BLOCK, i * RG_BLOCK))
        return m

    return jnp.stack([jnp.concatenate([bd(w_a, j), bd(w_x, j)], axis=1) for j in range(2)]).astype(BF16)


def kernel(x, c, ada_w, ada_b, ln_g, ln_b, ev_w_in, ev_conv_w, ev_conv_b, ev_rg_wa, ev_rg_ba, ev_rg_wx,
           ev_rg_bx, ev_rg_lam, ev_gla_w_up, ev_gla_b_up, ev_gla_norm_g, ev_w_out, od_w_in, od_b_f,
           od_q_norm_g, od_k_norm_g, od_w_out, mlp_w1, mlp_b1, mlp_w2, mlp_b2):
    bsz = x.shape[0]
    mods = _ada_call(c, ada_w.reshape(2 * DEPTH, D_MODEL, 3 * D_MODEL),
                     ada_b.reshape(2 * DEPTH, 1, 3 * D_MODEL))
    mods = mods.reshape(2 * DEPTH, bsz, 1, 3 * D_MODEL)
    place = jnp.asarray(_fox_place_matrix(), BF16)

    for layer in range(DEPTH):
        mod_mix = mods[2 * layer]
        mod_mlp = mods[2 * layer + 1]
        mlp_params = [mlp_w1[layer].astype(BF16), mlp_b1[layer][None], mlp_w2[layer].astype(BF16),
                      mlp_b2[layer][None], ln_g[layer, 1][None], ln_b[layer, 1][None]]
        if layer % 2 == 0:
            e = layer // 2
            w_up = jnp.pad(ev_gla_w_up[e], ((0, LANES - GLA_LOWRANK), (0, 0)))
            mixer_params = [
                _pad_cols(ev_w_in[e], EVEN_IN_PAD).astype(BF16), ev_conv_w[e], ev_conv_b[e][None],
                _gate_weights(ev_rg_wa[e], ev_rg_wx[e]), ev_rg_ba[e][None], ev_rg_bx[e][None],
                ev_rg_lam[e][None], w_up, ev_gla_b_up[e][None], ev_gla_norm_g[e][None],
                ev_w_out[e].astype(BF16), ln_g[layer, 0][None], ln_b[layer, 0][None]]
            x = _even_layer_call(x, mod_mix, mixer_params, mod_mlp, mlp_params)
        else:
            o = layer // 2
            w_in = _pad_cols(od_w_in[o], ODD_IN_PAD).astype(BF16)
            b_f = jnp.pad(od_b_f[o], (0, LANES - FOX_HEADS))[None]
            gq = jnp.tile(od_q_norm_g[o], 2)[None]
            gk = jnp.tile(od_k_norm_g[o], 2)[None]
            qa, ka, vt, og = _fox_prep_call(x, mod_mix, w_in, b_f, gq, gk, place)
            mix = _fox_attn_call(qa, ka, vt, og)
            x = _post_call(mix, od_w_out[o].astype(BF16), x, mod_mix, ln_g[layer, 0][None],
                           ln_b[layer, 0][None], mod_mlp, mlp_params)
    return x
```
